```python
import jax, jax.numpy as jnp
from jax import lax
import numpy as np


D_MODEL = 1024
BATCH = 16
SEQ = 4096
DEPTH = 4

CTX_LEN = 256
GRID_W = 64
W_CONV = 512
W_FNO = 512
N_FGROUPS = 4
N_HEADS = 8
HEAD_DIM = 64
W_ATTN = N_HEADS * HEAD_DIM
WIN_H = 8
WIN_W = 16
Q_BLOCK_W = 16
K_BLOCK_W = 2 * WIN_W
N_BRANCH = 3
EPS = 1e-6
NEG_INF = -1e30
D_IN = 4 * W_CONV + 2 * W_FNO + 4 * W_ATTN + N_BRANCH * D_MODEL
KV_LO = 4 * W_CONV + 2 * W_FNO + W_ATTN
KV_HI = KV_LO + 2 * W_ATTN

kernel_name = 'hybrid_conv_fourier_natten_dit'


def rmsnorm(x, g):
    xf = x.astype(jnp.float32)
    y = xf * lax.rsqrt(jnp.mean(xf * xf, axis=-1, keepdims=True) + EPS)
    return (y * g.astype(jnp.float32)).astype(x.dtype)


def split_proj(p):
    sizes = (W_CONV,) * 4 + (W_FNO,) * 2 + (W_ATTN,) * 4 + (D_MODEL,) * N_BRANCH
    points = np.cumsum(sizes)[:-1].tolist()
    return jnp.split(p, points, axis=-1)


def heads(t):
    return t.reshape(t.shape[0], t.shape[1], N_HEADS, HEAD_DIM)


def short_conv(u, w_conv, b_conv):
    up = jnp.pad(u, ((0, 0), (1, 1), (0, 0)))
    return up[:, :-2] * w_conv[0] + up[:, 1:-1] * w_conv[1] + up[:, 2:] * w_conv[2] + b_conv


def fourier_mix(u):
    b, l, w = u.shape
    ug = u.reshape(b, l, N_FGROUPS, w // N_FGROUPS).astype(jnp.float32)
    y = jnp.fft.fft2(ug, axes=(1, 3), norm='ortho').real
    return y.reshape(b, l, w).astype(u.dtype)


def context_attention(q, k, v):
    s = jnp.einsum('bqhd,bkhd->bhqk', q, k).astype(jnp.float32) * (HEAD_DIM ** -0.5)
    p = jax.nn.softmax(s, axis=-1).astype(v.dtype)
    o = jnp.einsum('bhqk,bkhd->bqhd', p, v)
    return o.reshape(o.shape[0], o.shape[1], W_ATTN)


def neighbourhood_attention(q, k, v, k_ctx, v_ctx, rpb):
    b, s, h, dh = q.shape
    rows = s // GRID_W
    kh = min(WIN_H, rows)
    n_cb = GRID_W // Q_BLOCK_W
    scale = HEAD_DIM ** -0.5
    q = q.reshape(b, rows, n_cb, Q_BLOCK_W, h, dh)
    k = k.reshape(b, rows, GRID_W, h, dh)
    v = v.reshape(b, rows, GRID_W, h, dh)
    qcol = np.arange(GRID_W).reshape(n_cb, Q_BLOCK_W)
    kcs = np.clip(np.arange(n_cb) * Q_BLOCK_W - WIN_W // 2, 0, GRID_W - K_BLOCK_W)
    kcol = kcs[:, None] + np.arange(K_BLOCK_W)[None, :]
    cs = np.clip(qcol - WIN_W // 2, 0, GRID_W - WIN_W)
    col_ok = (kcol[:, None, :] >= cs[..., None]) & (kcol[:, None, :] < cs[..., None] + WIN_W)
    dc_idx = np.clip(kcol[:, None, :] - qcol[..., None] + WIN_W - 1, 0, 2 * WIN_W - 2)
    rpb_col = rpb[:, :, dc_idx]
    mask = col_ok[:, :, None, :]

    def row_block(r):
        rs = jnp.clip(r - kh // 2, 0, rows - kh)
        qr = lax.dynamic_index_in_dim(q, r, axis=1, keepdims=False)
        kb = lax.dynamic_slice_in_dim(k, rs, kh, axis=1)[:, :, kcol]
        vb = lax.dynamic_slice_in_dim(v, rs, kh, axis=1)[:, :, kcol]
        dr_idx = rs + jnp.arange(kh) - r + (WIN_H - 1)
        bias = jnp.take(rpb_col, dr_idx, axis=1).transpose(0, 2, 3, 1, 4).astype(jnp.float32)
        s_loc = jnp.einsum('bnqhd,binchd->bhnqic', qr, kb).astype(jnp.float32) * scale + bias
        s_loc = jnp.where(mask, s_loc, NEG_INF)
        s_ctx = jnp.einsum('bnqhd,bkhd->bhnqk', qr, k_ctx).astype(jnp.float32) * scale
        n_loc = kh * K_BLOCK_W
        sc = jnp.concatenate([s_loc.reshape(b, h, n_cb, Q_BLOCK_W, n_loc), s_ctx], axis=-1)
        p = jax.nn.softmax(sc, axis=-1).astype(v.dtype)
        p_loc = p[..., :n_loc].reshape(b, h, n_cb, Q_BLOCK_W, kh, K_BLOCK_W)
        p_ctx = p[..., n_loc:]
        return (jnp.einsum('bhnqic,binchd->bnqhd', p_loc, vb)
                + jnp.einsum('bhnqk,bkhd->bnqhd', p_ctx, v_ctx))

    o = lax.map(row_block, jnp.arange(rows))
    return o.transpose(1, 0, 2, 3, 4, 5).reshape(b, s, W_ATTN)


def merge_branches(pieces, attn_o, w_conv, b_conv, w_br_a, w_br_f, w_br_c, w_out):
    a_x, a_b, a_c, a_z, f_u, f_z, _q, _k, _v, c_z, g_a, g_f, g_c = pieces
    silu, sig = jax.nn.silu, jax.nn.sigmoid
    y_a = (a_b * short_conv(a_c * a_x, w_conv, b_conv) * silu(a_z)) @ w_br_a
    y_f = (fourier_mix(f_u) * silu(f_z)) @ w_br_f
    y_c = (attn_o * silu(c_z)) @ w_br_c
    return (sig(g_a) * y_a + sig(g_f) * y_f + sig(g_c) * y_c) @ w_out


def hybrid_layer(x, ctx, c, c_ctx, norm_g, w_ada, b_ada, w_in, b_in, w_conv, b_conv,
                 rpb, w_br_a, w_br_f, w_br_c, w_out, update_ctx):
    mod_l = jax.nn.silu(c) @ w_ada + b_ada
    mod_c = jax.nn.silu(c_ctx) @ w_ada + b_ada
    sh_l, sc_l, gt_l = jnp.split(mod_l[:, None, :], 3, axis=-1)
    sh_c, sc_c, gt_c = jnp.split(mod_c, 3, axis=-1)
    h_lat = rmsnorm(x, norm_g) * (1.0 + sc_l) + sh_l
    h_ctx = rmsnorm(ctx, norm_g) * (1.0 + sc_c) + sh_c

    pl = split_proj(h_lat @ w_in + b_in)
    if update_ctx:
        pc = split_proj(h_ctx @ w_in + b_in)
        k_c, v_c = pc[7], pc[8]
    else:
        k_c, v_c = jnp.split(h_ctx @ w_in[:, KV_LO:KV_HI] + b_in[KV_LO:KV_HI], 2, axis=-1)
    k_c, v_c = heads(k_c), heads(v_c)

    o_lat = neighbourhood_attention(heads(pl[6]), heads(pl[7]), heads(pl[8]), k_c, v_c, rpb)
    x_new = x + gt_l * merge_branches(pl, o_lat, w_conv, b_conv, w_br_a, w_br_f, w_br_c, w_out)
    if update_ctx:
        o_ctx = context_attention(heads(pc[6]), k_c, v_c)
        ctx = ctx + gt_c * merge_branches(pc, o_ctx, w_conv, b_conv, w_br_a, w_br_f, w_br_c, w_out)
    return x_new, ctx


def setup_inputs(seed: int = 0) -> dict:
    key = jax.random.key(seed)
    ks = jax.random.split(key, 17)
    f32 = jnp.float32
    nrm = lambda k, shape, s: jax.random.normal(k, shape, f32) * s
    return {
        'x': nrm(ks[0], (BATCH, SEQ, D_MODEL), 1.0),
        'c': nrm(ks[1], (BATCH, D_MODEL), 1.0),
        'ctx': nrm(ks[2], (BATCH, CTX_LEN, D_MODEL), 1.0),
        'c_ctx': nrm(ks[3], (D_MODEL,), 1.0),
        'norm_g': 1.0 + nrm(ks[4], (DEPTH, D_MODEL), 0.05),
        'w_ada': nrm(ks[5], (DEPTH, D_MODEL, 3 * D_MODEL), 0.5 * D_MODEL ** -0.5),
        'b_ada': nrm(ks[6], (DEPTH, 3 * D_MODEL), 0.01),
        'w_in': nrm(ks[7], (DEPTH, D_MODEL, D_IN), D_MODEL ** -0.5),
        'b_in': nrm(ks[8], (DEPTH, D_IN), 0.01),
        'w_conv': nrm(ks[9], (DEPTH, 3, W_CONV), 3 ** -0.5),
        'b_conv': nrm(ks[10], (DEPTH, W_CONV), 0.01),
        'rpb': nrm(ks[11], (DEPTH, N_HEADS, 2 * WIN_H - 1, 2 * WIN_W - 1), 0.1),
        'w_br_a': nrm(ks[12], (DEPTH, W_CONV, D_MODEL), W_CONV ** -0.5),
        'w_br_f': nrm(ks[13], (DEPTH, W_FNO, D_MODEL), W_FNO ** -0.5),
        'w_br_c': nrm(ks[14], (DEPTH, W_ATTN, D_MODEL), W_ATTN ** -0.5),
        'w_out': nrm(ks[15], (DEPTH, D_MODEL, D_MODEL), D_MODEL ** -0.5),
        'final_g': 1.0 + nrm(ks[16], (D_MODEL,), 0.05),
    }


def reference(x, c, ctx, c_ctx, norm_g, w_ada, b_ada, w_in, b_in, w_conv, b_conv,
              rpb, w_br_a, w_br_f, w_br_c, w_out, final_g):
    for l in range(DEPTH):
        x, ctx = hybrid_layer(x, ctx, c, c_ctx, norm_g[l], w_ada[l], b_ada[l], w_in[l], b_in[l],
                              w_conv[l], b_conv[l], rpb[l], w_br_a[l], w_br_f[l], w_br_c[l],
                              w_out[l], update_ctx=(l < DEPTH - 1))
    return rmsnorm(x, final_g)
```

```python
import functools

import numpy as np
import jax
import jax.numpy as jnp
from jax import lax
from jax.experimental import pallas as pl
from jax.experimental.pallas import tpu as pltpu

F32 = jnp.float32
BF16 = jnp.bfloat16

D_MODEL = 1024
BRANCH_W = 512
N_FGROUPS = 4
FGROUP_W = BRANCH_W // N_FGROUPS
N_HEADS = 8
HEAD_DIM = 64
HEADS_PER_PACK = 4
PACK_W = HEADS_PER_PACK * HEAD_DIM
N_PACKS = N_HEADS // HEADS_PER_PACK
GRID_W = 64
WIN_H = 8
WIN_W = 16
EPS = 1e-6
NEG_INF = -1e30
D_IN = 8192
KV_LO = 3584
KV_HI = 4608

P_U, P_ABZ, P_FU, P_SFZ, P_Q, P_K, P_V, P_SCZ = range(8)
P_GATES = 8
P_COLS = 8 * BRANCH_W + 3 * D_MODEL

TOKEN_TILE = 512
HALO_ROWS = 16
ATTN_ROWS_PER_STEP = 4
DFT_ROW_TILE = 512
VMEM_LIMIT = 56 * 1024 * 1024


def _silu(v):
    return v * jax.nn.sigmoid(v)


def _params(n_grid_dims, vmem=VMEM_LIMIT):
    return pltpu.CompilerParams(dimension_semantics=("arbitrary",) * n_grid_dims,
                                vmem_limit_bytes=vmem)


def _mods_kernel(c_ref, w_ref, b_ref, o_ref):
    s = _silu(c_ref[...])
    o_ref[0] = jnp.dot(s, w_ref[0], preferred_element_type=F32,
                       precision=lax.Precision.HIGHEST) + b_ref[0]


def _mods_call(cc, w_ada, b_ada):
    depth = w_ada.shape[0]
    rows = cc.shape[0]
    return pl.pallas_call(
        _mods_kernel,
        grid=(depth, 3),
        in_specs=[
            pl.BlockSpec((rows, D_MODEL), lambda l, j: (0, 0)),
            pl.BlockSpec((1, D_MODEL, D_MODEL), lambda l, j: (l, 0, j)),
            pl.BlockSpec((1, 1, D_MODEL), lambda l, j: (l, 0, j)),
        ],
        out_specs=pl.BlockSpec((1, rows, D_MODEL), lambda l, j: (l, 0, j)),
        out_shape=jax.ShapeDtypeStruct((depth, rows, 3 * D_MODEL), F32),
        compiler_params=_params(2),
        name="adaln_mods",
    )(cc, w_ada, b_ada.reshape(depth, 1, 3 * D_MODEL))


def _inproj_kernel(x_ref, g_ref, sh_ref, sc_ref, w_ref, b_ref, o_ref, h_ref, *, full):
    x = x_ref[0]
    y = x * lax.rsqrt(jnp.mean(x * x, axis=-1, keepdims=True) + EPS) * g_ref[...]
    h_ref[...] = (y * (1.0 + sc_ref[0]) + sh_ref[0]).astype(BF16)

    def proj(c):
        cols = slice(c * BRANCH_W, (c + 1) * BRANCH_W)
        return jnp.dot(h_ref[...], w_ref[:, cols], preferred_element_type=F32) + b_ref[:, cols]

    def put(c, v):
        o_ref[0, :, c * BRANCH_W:(c + 1) * BRANCH_W] = v.astype(BF16)

    if not full:
        put(0, proj(0))
        put(1, proj(1))
        return
    put(P_U, proj(0) * proj(2))
    put(P_ABZ, proj(1) * _silu(proj(3)))
    put(P_FU, proj(4))
    put(P_SFZ, _silu(proj(5)))
    put(P_Q, proj(6))
    put(P_K, proj(7))
    put(P_V, proj(8))
    put(P_SCZ, _silu(proj(9)))
    for c in range(6):
        put(P_GATES + c, jax.nn.sigmoid(proj(10 + c)))


def _inproj_call(x, norm_g, mods, mod_row0, w, b, *, full):
    nb, s, _ = x.shape
    n_in = w.shape[1]
    n_out = P_COLS if full else n_in
    tm = TOKEN_TILE
    mod_spec = lambda j: pl.BlockSpec((1, 1, D_MODEL), lambda bi, i: (bi + mod_row0, 0, j))
    return pl.pallas_call(
        functools.partial(_inproj_kernel, full=full),
        grid=(nb, s // tm),
        in_specs=[
            pl.BlockSpec((1, tm, D_MODEL), lambda bi, i: (bi, i, 0)),
            pl.BlockSpec((1, D_MODEL), lambda bi, i: (0, 0)),
            mod_spec(0),
            mod_spec(1),
            pl.BlockSpec((D_MODEL, n_in), lambda bi, i: (0, 0), pipeline_mode=pl.Buffered(1)),
            pl.BlockSpec((1, n_in), lambda bi, i: (0, 0)),
        ],
        out_specs=pl.BlockSpec((1, tm, n_out), lambda bi, i: (bi, i, 0)),
        out_shape=jax.ShapeDtypeStruct((nb, s, n_out), BF16),
        scratch_shapes=[pltpu.VMEM((tm, D_MODEL), BF16)],
        compiler_params=_params(2),
        name="inproj_full" if full else "inproj_kv",
    )(x, norm_g.reshape(1, D_MODEL), mods, mods, w, b.reshape(1, n_in))


def _dft_tables(seq_len):
    def angles(n):
        k = lax.broadcasted_iota(jnp.int32, (n, n), 0)
        m = lax.broadcasted_iota(jnp.int32, (n, n), 1)
        return ((k * m) % n).astype(F32) * (2.0 * np.pi / n)
    ac = angles(FGROUP_W)
    chan = jnp.concatenate([jnp.cos(ac), jnp.sin(ac)], axis=1) * (FGROUP_W ** -0.5)
    ap = angles(seq_len)
    pos = jnp.concatenate([jnp.cos(ap), -jnp.sin(ap)], axis=1) * (seq_len ** -0.5)
    return chan.astype(BF16), pos.astype(BF16)


def _fourier_kernel(fu_ref, fz_ref, chan_ref, pos_ref, o_ref, v_ref, *, seq_len, chunk):
    @pl.when(pl.program_id(1) == 0)
    def _():
        for g in range(N_FGROUPS):
            cols = slice(g * FGROUP_W, (g + 1) * FGROUP_W)
            for r0 in range(0, seq_len, chunk):
                t = jnp.dot(fu_ref[0, r0:r0 + chunk, cols], chan_ref[...],
                            preferred_element_type=F32)
                v_ref[r0:r0 + chunk, cols] = t[:, :FGROUP_W].astype(BF16)
                v_ref[seq_len + r0:seq_len + r0 + chunk, cols] = t[:, FGROUP_W:].astype(BF16)

    y = jnp.dot(pos_ref[...], v_ref[...], preferred_element_type=F32)
    o_ref[0] = (y * fz_ref[0].astype(F32)).astype(BF16)


def _fourier_call(p, seq_len, tables):
    n_seq = p.shape[0]
    chan, pos = tables
    tk = min(DFT_ROW_TILE, seq_len)
    return pl.pallas_call(
        functools.partial(_fourier_kernel, seq_len=seq_len, chunk=min(512, seq_len)),
        grid=(n_seq, seq_len // tk),
        in_specs=[
            pl.BlockSpec((1, seq_len, BRANCH_W), lambda b, j: (b, 0, P_FU)),
            pl.BlockSpec((1, tk, BRANCH_W), lambda b, j: (b, j, P_SFZ)),
            pl.BlockSpec((FGROUP_W, 2 * FGROUP_W), lambda b, j: (0, 0)),
            pl.BlockSpec((tk, 2 * seq_len), lambda b, j: (j, 0)),
        ],
        out_specs=pl.BlockSpec((1, tk, BRANCH_W), lambda b, j: (b, j, 0)),
        out_shape=jax.ShapeDtypeStruct((n_seq, seq_len, BRANCH_W), BF16),
        scratch_shapes=[pltpu.VMEM((2 * seq_len, BRANCH_W), BF16)],
        compiler_params=_params(2),
        name=f"fourier_{seq_len}",
    )(p, p, chan, pos)


def _lane_head():
    return lax.broadcasted_iota(jnp.int32, (1, PACK_W), 1) // HEAD_DIM


def _stack_heads(q_pack):
    lane_head = _lane_head()
    qs = q_pack * jnp.asarray(HEAD_DIM ** -0.5, q_pack.dtype)
    zero = jnp.zeros_like(qs)
    return jnp.concatenate([jnp.where(lane_head == h, qs, zero) for h in range(HEADS_PER_PACK)],
                           axis=0)


def _unstack_heads(o_stack, n):
    lane_head = _lane_head()
    out = jnp.zeros((n, PACK_W), F32)
    for h in range(HEADS_PER_PACK):
        out = jnp.where(lane_head == h, o_stack[h * n:(h + 1) * n], out)
    return out


def _nt_dot(a, b):
    return lax.dot_general(a, b, (((1,), (1,)), ((), ())), preferred_element_type=F32)


def _softmax_pv(scores, values):
    m = functools.reduce(jnp.maximum, [jnp.max(s, axis=-1, keepdims=True) for s in scores])
    es = [jnp.exp(s - m) for s in scores]
    denom = functools.reduce(jnp.add, [jnp.sum(e, axis=-1, keepdims=True) for e in es])
    acc = functools.reduce(
        jnp.add, [jnp.dot(e.astype(BF16), v, preferred_element_type=F32) for e, v in zip(es, values)])
    return acc / denom


def _nbr_attn_kernel(q_ref, k_ref, v_ref, kc_ref, vc_ref, bias_ref, z_ref, o_ref, *, n_rows):
    step = pl.program_id(1)
    win = WIN_H * GRID_W

    def one_row(j, carry):
        r = step * ATTN_ROWS_PER_STEP + j
        rs = jnp.clip(r - WIN_H // 2, 0, n_rows - WIN_H)
        delta = r - rs
        q0 = pl.multiple_of(j * GRID_W, GRID_W)
        k0 = pl.multiple_of(rs * GRID_W, GRID_W)
        for pk in range(N_PACKS):
            cols = slice(pk * PACK_W, (pk + 1) * PACK_W)
            qm = _stack_heads(q_ref[0, pl.ds(q0, GRID_W), cols])
            s_loc = _nt_dot(qm, k_ref[0, pl.ds(k0, win), cols]) + bias_ref[delta, pk]
            s_ctx = _nt_dot(qm, kc_ref[0, :, cols])
            o = _softmax_pv([s_loc, s_ctx], [v_ref[0, pl.ds(k0, win), cols], vc_ref[0, :, cols]])
            o = _unstack_heads(o, GRID_W)
            z = z_ref[0, pl.ds(q0, GRID_W), cols].astype(F32)
            o_ref[0, pl.ds(q0, GRID_W), cols] = (o * z).astype(BF16)
        return carry

    lax.fori_loop(0, ATTN_ROWS_PER_STEP, one_row, 0)


def _nbr_bias_table(rpb):
    qcol = np.arange(GRID_W)[:, None]
    kcol = np.arange(GRID_W)[None, :]
    cs = np.clip(qcol - WIN_W // 2, 0, GRID_W - WIN_W)
    col_ok = (kcol >= cs) & (kcol < cs + WIN_W)
    dc_idx = np.clip(kcol - qcol + WIN_W - 1, 0, 2 * WIN_W - 2)
    rpb_col = rpb[:, :, dc_idx]
    per_delta = jnp.stack([rpb_col[:, WIN_H - 1 - d:2 * WIN_H - 1 - d] for d in range(WIN_H)])
    t = per_delta.transpose(0, 1, 3, 2, 4)
    t = jnp.where(col_ok[:, None, :], t, NEG_INF)
    return t.reshape(WIN_H, N_PACKS, HEADS_PER_PACK * GRID_W, WIN_H * GRID_W).astype(F32)


def _nbr_attn_call(p, kv_ctx, kc_blk, vc_blk, bias):
    nb, s, _ = p.shape
    ctx_len = kv_ctx.shape[1]
    n_rows = s // GRID_W
    tq = ATTN_ROWS_PER_STEP * GRID_W
    return pl.pallas_call(
        functools.partial(_nbr_attn_kernel, n_rows=n_rows),
        grid=(nb, n_rows // ATTN_ROWS_PER_STEP),
        in_specs=[
            pl.BlockSpec((1, tq, BRANCH_W), lambda b, i: (b, i, P_Q)),
            pl.BlockSpec((1, s, BRANCH_W), lambda b, i: (b, 0, P_K)),
            pl.BlockSpec((1, s, BRANCH_W), lambda b, i: (b, 0, P_V)),
            pl.BlockSpec((1, ctx_len, BRANCH_W), lambda b, i: (b, 0, kc_blk)),
            pl.BlockSpec((1, ctx_len, BRANCH_W), lambda b, i: (b, 0, vc_blk)),
            pl.BlockSpec(bias.shape, lambda b, i: (0, 0, 0, 0), pipeline_mode=pl.Buffered(1)),
            pl.BlockSpec((1, tq, BRANCH_W), lambda b, i: (b, i, P_SCZ)),
        ],
        out_specs=pl.BlockSpec((1, tq, BRANCH_W), lambda b, i: (b, i, 0)),
        out_shape=jax.ShapeDtypeStruct((nb, s, BRANCH_W), BF16),
        compiler_params=_params(2),
        name="nbr_attention",
    )(p, p, p, kv_ctx, kv_ctx, bias, p)


def _ctx_attn_kernel(q_ref, k_ref, v_ref, z_ref, o_ref, *, ctx_len):
    for q0 in range(0, ctx_len, GRID_W):
        for pk in range(N_PACKS):
            cols = slice(pk * PACK_W, (pk + 1) * PACK_W)
            qm = _stack_heads(q_ref[0, q0:q0 + GRID_W, cols])
            s = _nt_dot(qm, k_ref[0, :, cols])
            o = _unstack_heads(_softmax_pv([s], [v_ref[0, :, cols]]), GRID_W)
            z = z_ref[0, q0:q0 + GRID_W, cols].astype(F32)
            o_ref[0, q0:q0 + GRID_W, cols] = (o * z).astype(BF16)


def _ctx_attn_call(pc):
    nb, ctx_len, _ = pc.shape
    blk = lambda c: pl.BlockSpec((1, ctx_len, BRANCH_W), lambda b: (b, 0, c))
    return pl.pallas_call(
        functools.partial(_ctx_attn_kernel, ctx_len=ctx_len),
        grid=(nb,),
        in_specs=[blk(P_Q), blk(P_K), blk(P_V), blk(P_SCZ)],
        out_specs=blk(0),
        out_shape=jax.ShapeDtypeStruct((nb, ctx_len, BRANCH_W), BF16),
        compiler_params=_params(1),
        name="ctx_attention",
    )(pc, pc, pc, pc)


def _merge_kernel(x_ref, u_ref, up_ref, un_ref, abz_ref, yf_ref, yc_ref, ga_ref, gf_ref, gc_ref,
                  gt_ref, wcv_ref, bcv_ref, wa_ref, wf_ref, wc_ref, wo_ref, fg_ref, o_ref,
                  *, period, final):
    tm = u_ref.shape[1]
    u = u_ref[0].astype(F32)
    row = lax.broadcasted_iota(jnp.int32, (tm, 1), 0)
    pos = (pl.program_id(1) * tm + row) % period
    u_prev = pltpu.roll(u, 1, 0)
    u_prev = jnp.where(row == 0, up_ref[0, HALO_ROWS - 1:HALO_ROWS, :].astype(F32), u_prev)
    u_prev = jnp.where(pos == 0, 0.0, u_prev)
    u_next = pltpu.roll(u, tm - 1, 0)
    u_next = jnp.where(row == tm - 1, un_ref[0, 0:1, :].astype(F32), u_next)
    u_next = jnp.where(pos == period - 1, 0.0, u_next)
    conv = u_prev * wcv_ref[0:1, :] + u * wcv_ref[1:2, :] + u_next * wcv_ref[2:3, :] + bcv_ref[...]
    ya_in = (abz_ref[0].astype(F32) * conv).astype(BF16)

    y_a = jnp.dot(ya_in, wa_ref[...], preferred_element_type=F32)
    y_f = jnp.dot(yf_ref[0], wf_ref[...], preferred_element_type=F32)
    y_c = jnp.dot(yc_ref[0], wc_ref[...], preferred_element_type=F32)
    mix = (ga_ref[0].astype(F32) * y_a + gf_ref[0].astype(F32) * y_f
           + gc_ref[0].astype(F32) * y_c)
    z = jnp.dot(mix.astype(BF16), wo_ref[...], preferred_element_type=F32)
    xn = x_ref[0] + gt_ref[0] * z
    if final:
        xn = xn * lax.rsqrt(jnp.mean(xn * xn, axis=-1, keepdims=True) + EPS) * fg_ref[...]
    o_ref[0] = xn


def _merge_call(x, p, yf, yc, mods, mod_row0, w_conv, b_conv, w_a, w_f, w_c, w_o, final_g,
                *, period, final):
    nb, s, _ = x.shape
    tm = TOKEN_TILE
    halo_per_tile = tm // HALO_ROWS
    n_halo = s // HALO_ROWS
    tok = lambda width, c: pl.BlockSpec((1, tm, width), lambda b, i: (b, i, c))
    const = lambda shape: pl.BlockSpec(shape, lambda b, i: (0,) * len(shape))
    return pl.pallas_call(
        functools.partial(_merge_kernel, period=period, final=final),
        grid=(nb, s // tm),
        in_specs=[
            tok(D_MODEL, 0),
            tok(BRANCH_W, P_U),
            pl.BlockSpec((1, HALO_ROWS, BRANCH_W),
                         lambda b, i: (b, jnp.maximum(i * halo_per_tile - 1, 0), P_U)),
            pl.BlockSpec((1, HALO_ROWS, BRANCH_W),
                         lambda b, i: (b, jnp.minimum((i + 1) * halo_per_tile, n_halo - 1), P_U)),
            tok(BRANCH_W, P_ABZ),
            tok(BRANCH_W, 0),
            tok(BRANCH_W, 0),
            tok(D_MODEL, P_GATES // 2),
            tok(D_MODEL, P_GATES // 2 + 1),
            tok(D_MODEL, P_GATES // 2 + 2),
            pl.BlockSpec((1, 1, D_MODEL), lambda b, i: (b + mod_row0, 0, 2)),
            const((3, BRANCH_W)),
            const((1, BRANCH_W)),
            const((BRANCH_W, D_MODEL)),
            const((BRANCH_W, D_MODEL)),
            const((BRANCH_W, D_MODEL)),
            const((D_MODEL, D_MODEL)),
            const((1, D_MODEL)),
        ],
        out_specs=tok(D_MODEL, 0),
        out_shape=jax.ShapeDtypeStruct((nb, s, D_MODEL), F32),
        compiler_params=_params(2),
        name="merge_final" if final else "merge",
    )(x, p, p, p, p, yf, yc, p, p, p, mods, w_conv, b_conv.reshape(1, BRANCH_W),
      w_a, w_f, w_c, w_o, final_g.reshape(1, D_MODEL))


def kernel(x, c, ctx, c_ctx, norm_g, w_ada, b_ada, w_in, b_in, w_conv, b_conv, rpb,
           w_br_a, w_br_f, w_br_c, w_out, final_g):
    nb, s, _ = x.shape
    ctx_len = ctx.shape[1]
    depth = w_in.shape[0]

    n_cond = -(-(nb + 1) // 8) * 8
    cc = jnp.zeros((n_cond, D_MODEL), F32).at[:nb].set(c).at[nb].set(c_ctx)
    mods = _mods_call(cc, w_ada, b_ada).reshape(depth, n_cond, 1, 3 * D_MODEL)

    w_in_b = w_in.astype(BF16)
    w_a_b, w_f_b, w_c_b, w_o_b = (w.astype(BF16) for w in (w_br_a, w_br_f, w_br_c, w_out))
    tables_lat = _dft_tables(s)
    tables_ctx = _dft_tables(ctx_len)

    xc = ctx.reshape(1, nb * ctx_len, D_MODEL)
    for l in range(depth):
        last = l == depth - 1
        m = mods[l]
        bias = _nbr_bias_table(rpb[l])
        p_lat = _inproj_call(x, norm_g[l], m, 0, w_in_b[l], b_in[l], full=True)
        if last:
            kv = _inproj_call(xc, norm_g[l], m, nb, w_in_b[l][:, KV_LO:KV_HI],
                              b_in[l][KV_LO:KV_HI], full=False)
            kv_ctx, kc_blk, vc_blk = kv.reshape(nb, ctx_len, KV_HI - KV_LO), 0, 1
        else:
            p_ctx = _inproj_call(xc, norm_g[l], m, nb, w_in_b[l], b_in[l], full=True)
            kv_ctx, kc_blk, vc_blk = p_ctx.reshape(nb, ctx_len, P_COLS), P_K, P_V

        yf = _fourier_call(p_lat, s, tables_lat)
        yc = _nbr_attn_call(p_lat, kv_ctx, kc_blk, vc_blk, bias)
        x = _merge_call(x, p_lat, yf, yc, m, 0, w_conv[l], b_conv[l], w_a_b[l], w_f_b[l],
                        w_c_b[l], w_o_b[l], final_g, period=s, final=last)
        if not last:
            yf_c = _fourier_call(kv_ctx, ctx_len, tables_ctx).reshape(1, nb * ctx_len, BRANCH_W)
            yc_c = _ctx_attn_call(kv_ctx).reshape(1, nb * ctx_len, BRANCH_W)
            xc = _merge_call(xc, p_ctx, yf_c, yc_c, m, nb, w_conv[l], b_conv[l], w_a_b[l],
                             w_f_b[l], w_c_b[l], w_o_b[l], final_g, period=ctx_len, final=False)
    return x
```

```python
import functools

import numpy as np
import jax
import jax.numpy as jnp
from jax import lax
from jax.experimental import pallas as pl
from jax.experimental.pallas import tpu as pltpu

F32 = jnp.float32
BF16 = jnp.bfloat16

D_MODEL = 1024
BRANCH_W = 512
N_FGROUPS = 4
FGROUP_W = BRANCH_W // N_FGROUPS
N_HEADS = 8
HEAD_DIM = 64
HEADS_PER_PACK = 4
PACK_W = HEADS_PER_PACK * HEAD_DIM
N_PACKS = N_HEADS // HEADS_PER_PACK
GRID_W = 64
WIN_H = 8
WIN_W = 16
EPS = 1e-6
NEG_INF = -1e30
LOG2_E = float(np.log2(np.e))
D_IN = 8192
Q_LO = 3072
KV_LO = 3584
KV_HI = 4608

P_U, P_ABZ, P_FU, P_SFZ, P_Q, P_K, P_V, P_SCZ = range(8)
P_GATES = 8
P_COLS = 8 * BRANCH_W + 3 * D_MODEL

TOKEN_TILE = 512
HALO_ROWS = 16
ATTN_ROWS_PER_STEP = 4
DFT_ROW_TILE = 512
DFT_FOLD_TILE = 256
VMEM_LIMIT = 56 * 1024 * 1024


def _silu(v):
    return v * jax.nn.sigmoid(v)


def _params(n_grid_dims, vmem=VMEM_LIMIT):
    return pltpu.CompilerParams(dimension_semantics=("arbitrary",) * n_grid_dims,
                                vmem_limit_bytes=vmem)


def _mods_kernel(c_ref, w_ref, b_ref, o_ref):
    s = _silu(c_ref[...])
    o_ref[0] = jnp.dot(s, w_ref[0], preferred_element_type=F32,
                       precision=lax.Precision.HIGHEST) + b_ref[0]


def _mods_call(cc, w_ada, b_ada):
    depth = w_ada.shape[0]
    rows = cc.shape[0]
    return pl.pallas_call(
        _mods_kernel,
        grid=(depth, 3),
        in_specs=[
            pl.BlockSpec((rows, D_MODEL), lambda l, j: (0, 0)),
            pl.BlockSpec((1, D_MODEL, D_MODEL), lambda l, j: (l, 0, j)),
            pl.BlockSpec((1, 1, D_MODEL), lambda l, j: (l, 0, j)),
        ],
        out_specs=pl.BlockSpec((1, rows, D_MODEL), lambda l, j: (l, 0, j)),
        out_shape=jax.ShapeDtypeStruct((depth, rows, 3 * D_MODEL), F32),
        compiler_params=_params(2),
        name="adaln_mods",
    )(cc, w_ada, b_ada.reshape(depth, 1, 3 * D_MODEL))


def _inproj_kernel(x_ref, g_ref, sh_ref, sc_ref, w_ref, b_ref, o_ref, h_ref, *, full):
    x = x_ref[0]
    y = x * lax.rsqrt(jnp.mean(x * x, axis=-1, keepdims=True) + EPS) * g_ref[...]
    h_ref[...] = (y * (1.0 + sc_ref[0]) + sh_ref[0]).astype(BF16)

    def proj(c):
        cols = slice(c * BRANCH_W, (c + 1) * BRANCH_W)
        return jnp.dot(h_ref[...], w_ref[:, cols], preferred_element_type=F32) + b_ref[:, cols]

    def put(c, v):
        o_ref[0, :, c * BRANCH_W:(c + 1) * BRANCH_W] = v.astype(BF16)

    if not full:
        put(0, proj(0))
        put(1, proj(1))
        return
    put(P_U, proj(0) * proj(2))
    put(P_ABZ, proj(1) * _silu(proj(3)))
    put(P_FU, proj(4))
    put(P_SFZ, _silu(proj(5)))
    put(P_Q, proj(6))
    put(P_K, proj(7))
    put(P_V, proj(8))
    put(P_SCZ, _silu(proj(9)))
    for c in range(6):
        put(P_GATES + c, jax.nn.sigmoid(proj(10 + c)))


def _inproj_call(x, norm_g, mods, mod_row0, w, b, *, full):
    nb, s, _ = x.shape
    n_in = w.shape[1]
    n_out = P_COLS if full else n_in
    tm = TOKEN_TILE
    mod_spec = lambda j: pl.BlockSpec((1, 1, D_MODEL), lambda bi, i: (bi + mod_row0, 0, j))
    return pl.pallas_call(
        functools.partial(_inproj_kernel, full=full),
        grid=(nb, s // tm),
        in_specs=[
            pl.BlockSpec((1, tm, D_MODEL), lambda bi, i: (bi, i, 0)),
            pl.BlockSpec((1, D_MODEL), lambda bi, i: (0, 0)),
            mod_spec(0),
            mod_spec(1),
            pl.BlockSpec((D_MODEL, n_in), lambda bi, i: (0, 0), pipeline_mode=pl.Buffered(1)),
            pl.BlockSpec((1, n_in), lambda bi, i: (0, 0)),
        ],
        out_specs=pl.BlockSpec((1, tm, n_out), lambda bi, i: (bi, i, 0)),
        out_shape=jax.ShapeDtypeStruct((nb, s, n_out), BF16),
        scratch_shapes=[pltpu.VMEM((tm, D_MODEL), BF16)],
        compiler_params=_params(2),
        name="inproj_full" if full else "inproj_kv",
    )(x, norm_g.reshape(1, D_MODEL), mods, mods, w, b.reshape(1, n_in))


def _dft_tables(seq_len):
    half = seq_len // 2
    ck = lax.broadcasted_iota(jnp.int32, (FGROUP_W, FGROUP_W), 0)
    cm = lax.broadcasted_iota(jnp.int32, (FGROUP_W, FGROUP_W), 1)
    ac = ((ck * cm) % FGROUP_W).astype(F32) * (2.0 * np.pi / FGROUP_W)
    zero = jnp.zeros_like(ac)
    chan = jnp.concatenate([jnp.concatenate([jnp.cos(ac), zero], axis=1),
                            jnp.concatenate([zero, jnp.sin(ac)], axis=1)], axis=0)
    chan = chan * (FGROUP_W ** -0.5)

    k = lax.broadcasted_iota(jnp.int32, (half, half), 0)
    l = lax.broadcasted_iota(jnp.int32, (half, half), 1)
    ang = ((k * l) % seq_len).astype(F32) * (2.0 * np.pi / seq_len)
    nyquist = jnp.where(k % 2 == 0, 1.0, -1.0)
    top = jnp.concatenate([jnp.cos(ang), jnp.where(l == 0, nyquist, -jnp.sin(ang))], axis=1)
    col_sign = jnp.where(lax.broadcasted_iota(jnp.int32, (1, seq_len), 1) % 2 == 0, 1.0, -1.0)
    pos = jnp.concatenate([top, top * col_sign], axis=0) * (seq_len ** -0.5)
    return chan.astype(BF16), pos.astype(BF16)


def _fourier_kernel(fu_ref, fz_ref, chan_ref, pos_ref, o_ref, v_ref, *, seq_len, tile):
    half = seq_len // 2

    @pl.when(pl.program_id(1) == 0)
    def _():
        flip = (lax.broadcasted_iota(jnp.int32, (tile, tile), 0)
                + lax.broadcasted_iota(jnp.int32, (tile, tile), 1) == tile).astype(BF16)
        row0 = lax.broadcasted_iota(jnp.int32, (tile, 1), 0) == 0
        for t in range(half // tile):
            lo = fu_ref[0, t * tile:(t + 1) * tile, :].astype(F32)
            hi0 = seq_len - (t + 1) * tile
            rev = jnp.dot(flip, fu_ref[0, hi0:hi0 + tile, :], preferred_element_type=F32)
            if t > 0:
                nxt = fu_ref[0, hi0 + tile:hi0 + tile + HALO_ROWS, :].astype(F32)
                rev = jnp.where(row0, nxt[0:1, :], rev)
            u_plus = (lo + rev).astype(BF16)
            u_minus = (lo - rev).astype(BF16)
            for g in range(N_FGROUPS):
                cols = slice(g * FGROUP_W, (g + 1) * FGROUP_W)
                ab = jnp.dot(jnp.concatenate([u_plus[:, cols], u_minus[:, cols]], axis=1),
                             chan_ref[...], preferred_element_type=F32)
                a, b = ab[:, :FGROUP_W], ab[:, FGROUP_W:]
                if t == 0:
                    u_mid = fu_ref[0, half:half + HALO_ROWS, cols]
                    a_mid = jnp.dot(jnp.concatenate([u_mid, jnp.zeros_like(u_mid)], axis=1),
                                    chan_ref[...], preferred_element_type=F32)
                    b = jnp.where(row0, a_mid[0:1, :FGROUP_W], b)
                v_ref[t * tile:(t + 1) * tile, cols] = a.astype(BF16)
                v_ref[half + t * tile:half + (t + 1) * tile, cols] = b.astype(BF16)

    y = jnp.dot(pos_ref[...], v_ref[...], preferred_element_type=F32)
    o_ref[0] = (y * fz_ref[0].astype(F32)).astype(BF16)


def _fourier_call(p, seq_len, tables):
    n_seq = p.shape[0]
    chan, pos = tables
    tk = min(DFT_ROW_TILE, seq_len)
    return pl.pallas_call(
        functools.partial(_fourier_kernel, seq_len=seq_len, tile=min(DFT_FOLD_TILE, seq_len // 2)),
        grid=(n_seq, seq_len // tk),
        in_specs=[
            pl.BlockSpec((1, seq_len, BRANCH_W), lambda b, j: (b, 0, P_FU)),
            pl.BlockSpec((1, tk, BRANCH_W), lambda b, j: (b, j, P_SFZ)),
            pl.BlockSpec((2 * FGROUP_W, 2 * FGROUP_W), lambda b, j: (0, 0)),
            pl.BlockSpec((tk, seq_len), lambda b, j: (j, 0)),
        ],
        out_specs=pl.BlockSpec((1, tk, BRANCH_W), lambda b, j: (b, j, 0)),
        out_shape=jax.ShapeDtypeStruct((n_seq, seq_len, BRANCH_W), BF16),
        scratch_shapes=[pltpu.VMEM((seq_len, BRANCH_W), BF16)],
        compiler_params=_params(2),
        name=f"fourier_{seq_len}",
    )(p, p, chan, pos)


def _lane_head():
    return lax.broadcasted_iota(jnp.int32, (1, PACK_W), 1) // HEAD_DIM


def _stack_heads(q_pack):
    lane_head = _lane_head()
    zero = jnp.zeros_like(q_pack)
    return jnp.concatenate([jnp.where(lane_head == h, q_pack, zero)
                            for h in range(HEADS_PER_PACK)], axis=0)


def _unstack_heads(o_stack, n):
    lane_head = _lane_head()
    out = jnp.zeros((n, PACK_W), F32)
    for h in range(HEADS_PER_PACK):
        out = jnp.where(lane_head == h, o_stack[h * n:(h + 1) * n], out)
    return out


def _nt_dot(a, b):
    return lax.dot_general(a, b, (((1,), (1,)), ((), ())), preferred_element_type=F32)


def _row_reduce(blocks, combine, reduce):
    chunks = [b[:, i:i + 128] for b in blocks for i in range(0, b.shape[1], 128)]
    return reduce(functools.reduce(combine, chunks), axis=-1, keepdims=True)


def _softmax_weights(scores):
    m = _row_reduce(scores, jnp.maximum, jnp.max)
    es = [jnp.exp2(s - m) for s in scores]
    return [e.astype(BF16) for e in es], _row_reduce(es, jnp.add, jnp.sum)


def _nbr_attn_kernel(q_ref, k_ref, v_ref, kc_ref, vc_ref, bias_ref, z_ref, o_ref, *, n_rows):
    step = pl.program_id(1)
    win = WIN_H * GRID_W
    for j in range(ATTN_ROWS_PER_STEP):
        r = step * ATTN_ROWS_PER_STEP + j
        rs = jnp.clip(r - WIN_H // 2, 0, n_rows - WIN_H)
        k0 = pl.multiple_of(rs * GRID_W, GRID_W)
        q_rows = slice(j * GRID_W, (j + 1) * GRID_W)
        for pk in range(N_PACKS):
            cols = slice(pk * PACK_W, (pk + 1) * PACK_W)
            qm = _stack_heads(q_ref[0, q_rows, cols])
            s_loc = _nt_dot(qm, k_ref[0, pl.ds(k0, win), cols]) + bias_ref[r - rs, pk]
            s_ctx = _nt_dot(qm, kc_ref[0, :, cols])
            (p_loc, p_ctx), denom = _softmax_weights([s_loc, s_ctx])
            o = (jnp.dot(p_loc, v_ref[0, pl.ds(k0, win), cols], preferred_element_type=F32)
                 + jnp.dot(p_ctx, vc_ref[0, :, cols], preferred_element_type=F32))
            o = _unstack_heads(o / denom, GRID_W)
            z = z_ref[0, q_rows, cols].astype(F32)
            o_ref[0, q_rows, cols] = (o * z).astype(BF16)


def _nbr_bias_table(rpb):
    qcol = np.arange(GRID_W)[:, None]
    kcol = np.arange(GRID_W)[None, :]
    cs = np.clip(qcol - WIN_W // 2, 0, GRID_W - WIN_W)
    col_ok = (kcol >= cs) & (kcol < cs + WIN_W)
    dc_idx = np.clip(kcol - qcol + WIN_W - 1, 0, 2 * WIN_W - 2)
    rpb_col = rpb[:, :, dc_idx]
    per_delta = jnp.stack([rpb_col[:, WIN_H - 1 - d:2 * WIN_H - 1 - d] for d in range(WIN_H)])
    t = per_delta.transpose(0, 1, 3, 2, 4)
    t = jnp.where(col_ok[:, None, :], t * LOG2_E, NEG_INF)
    return t.reshape(WIN_H, N_PACKS, HEADS_PER_PACK * GRID_W, WIN_H * GRID_W).astype(F32)


def _nbr_attn_call(p, kv_ctx, kc_blk, vc_blk, bias):
    nb, s, _ = p.shape
    ctx_len = kv_ctx.shape[1]
    n_rows = s // GRID_W
    tq = ATTN_ROWS_PER_STEP * GRID_W
    return pl.pallas_call(
        functools.partial(_nbr_attn_kernel, n_rows=n_rows),
        grid=(nb, n_rows // ATTN_ROWS_PER_STEP),
        in_specs=[
            pl.BlockSpec((1, tq, BRANCH_W), lambda b, i: (b, i, P_Q)),
            pl.BlockSpec((1, s, BRANCH_W), lambda b, i: (b, 0, P_K)),
            pl.BlockSpec((1, s, BRANCH_W), lambda b, i: (b, 0, P_V)),
            pl.BlockSpec((1, ctx_len, BRANCH_W), lambda b, i: (b, 0, kc_blk)),
            pl.BlockSpec((1, ctx_len, BRANCH_W), lambda b, i: (b, 0, vc_blk)),
            pl.BlockSpec(bias.shape, lambda b, i: (0, 0, 0, 0), pipeline_mode=pl.Buffered(1)),
            pl.BlockSpec((1, tq, BRANCH_W), lambda b, i: (b, i, P_SCZ)),
        ],
        out_specs=pl.BlockSpec((1, tq, BRANCH_W), lambda b, i: (b, i, 0)),
        out_shape=jax.ShapeDtypeStruct((nb, s, BRANCH_W), BF16),
        compiler_params=_params(2),
        name="nbr_attention",
    )(p, p, p, kv_ctx, kv_ctx, bias, p)


def _ctx_attn_kernel(q_ref, k_ref, v_ref, z_ref, o_ref, *, ctx_len):
    for q0 in range(0, ctx_len, GRID_W):
        for pk in range(N_PACKS):
            cols = slice(pk * PACK_W, (pk + 1) * PACK_W)
            qm = _stack_heads(q_ref[0, q0:q0 + GRID_W, cols])
            (p,), d = _softmax_weights([_nt_dot(qm, k_ref[0, :, cols])])
            o = jnp.dot(p, v_ref[0, :, cols], preferred_element_type=F32) / d
            o = _unstack_heads(o, GRID_W)
            z = z_ref[0, q0:q0 + GRID_W, cols].astype(F32)
            o_ref[0, q0:q0 + GRID_W, cols] = (o * z).astype(BF16)


def _ctx_attn_call(pc):
    nb, ctx_len, _ = pc.shape
    blk = lambda c: pl.BlockSpec((1, ctx_len, BRANCH_W), lambda b: (b, 0, c))
    return pl.pallas_call(
        functools.partial(_ctx_attn_kernel, ctx_len=ctx_len),
        grid=(nb,),
        in_specs=[blk(P_Q), blk(P_K), blk(P_V), blk(P_SCZ)],
        out_specs=blk(0),
        out_shape=jax.ShapeDtypeStruct((nb, ctx_len, BRANCH_W), BF16),
        compiler_params=_params(1),
        name="ctx_attention",
    )(pc, pc, pc, pc)


def _merge_kernel(x_ref, u_ref, up_ref, un_ref, abz_ref, yf_ref, yc_ref, ga_ref, gf_ref, gc_ref,
                  gt_ref, wcv_ref, bcv_ref, wa_ref, wf_ref, wc_ref, wo_ref, fg_ref, o_ref,
                  *, period, final):
    tm = u_ref.shape[1]
    u = u_ref[0].astype(F32)
    row = lax.broadcasted_iota(jnp.int32, (tm, 1), 0)
    pos = (pl.program_id(1) * tm + row) % period
    u_prev = pltpu.roll(u, 1, 0)
    u_prev = jnp.where(row == 0, up_ref[0, HALO_ROWS - 1:HALO_ROWS, :].astype(F32), u_prev)
    u_prev = jnp.where(pos == 0, 0.0, u_prev)
    u_next = pltpu.roll(u, tm - 1, 0)
    u_next = jnp.where(row == tm - 1, un_ref[0, 0:1, :].astype(F32), u_next)
    u_next = jnp.where(pos == period - 1, 0.0, u_next)
    conv = u_prev * wcv_ref[0:1, :] + u * wcv_ref[1:2, :] + u_next * wcv_ref[2:3, :] + bcv_ref[...]
    ya_in = (abz_ref[0].astype(F32) * conv).astype(BF16)

    y_a = jnp.dot(ya_in, wa_ref[...], preferred_element_type=F32)
    y_f = jnp.dot(yf_ref[0], wf_ref[...], preferred_element_type=F32)
    y_c = jnp.dot(yc_ref[0], wc_ref[...], preferred_element_type=F32)
    mix = (ga_ref[0].astype(F32) * y_a + gf_ref[0].astype(F32) * y_f
           + gc_ref[0].astype(F32) * y_c)
    z = jnp.dot(mix.astype(BF16), wo_ref[...], preferred_element_type=F32)
    xn = x_ref[0] + gt_ref[0] * z
    if final:
        xn = xn * lax.rsqrt(jnp.mean(xn * xn, axis=-1, keepdims=True) + EPS) * fg_ref[...]
    o_ref[0] = xn


def _merge_call(x, p, yf, yc, mods, mod_row0, w_conv, b_conv, w_a, w_f, w_c, w_o, final_g,
                *, period, final):
    nb, s, _ = x.shape
    tm = TOKEN_TILE
    halo_per_tile = tm // HALO_ROWS
    n_halo = s // HALO_ROWS
    tok = lambda width, c: pl.BlockSpec((1, tm, width), lambda b, i: (b, i, c))
    const = lambda shape: pl.BlockSpec(shape, lambda b, i: (0,) * len(shape))
    return pl.pallas_call(
        functools.partial(_merge_kernel, period=period, final=final),
        grid=(nb, s // tm),
        in_specs=[
            tok(D_MODEL, 0),
            tok(BRANCH_W, P_U),
            pl.BlockSpec((1, HALO_ROWS, BRANCH_W),
                         lambda b, i: (b, jnp.maximum(i * halo_per_tile - 1, 0), P_U)),
            pl.BlockSpec((1, HALO_ROWS, BRANCH_W),
                         lambda b, i: (b, jnp.minimum((i + 1) * halo_per_tile, n_halo - 1), P_U)),
            tok(BRANCH_W, P_ABZ),
            tok(BRANCH_W, 0),
            tok(BRANCH_W, 0),
            tok(D_MODEL, P_GATES // 2),
            tok(D_MODEL, P_GATES // 2 + 1),
            tok(D_MODEL, P_GATES // 2 + 2),
            pl.BlockSpec((1, 1, D_MODEL), lambda b, i: (b + mod_row0, 0, 2)),
            const((3, BRANCH_W)),
            const((1, BRANCH_W)),
            const((BRANCH_W, D_MODEL)),
            const((BRANCH_W, D_MODEL)),
            const((BRANCH_W, D_MODEL)),
            const((D_MODEL, D_MODEL)),
            const((1, D_MODEL)),
        ],
        out_specs=tok(D_MODEL, 0),
        out_shape=jax.ShapeDtypeStruct((nb, s, D_MODEL), F32),
        compiler_params=_params(2),
        name="merge_final" if final else "merge",
    )(x, p, p, p, p, yf, yc, p, p, p, mods, w_conv, b_conv.reshape(1, BRANCH_W),
      w_a, w_f, w_c, w_o, final_g.reshape(1, D_MODEL))


def kernel(x, c, ctx, c_ctx, norm_g, w_ada, b_ada, w_in, b_in, w_conv, b_conv, rpb,
           w_br_a, w_br_f, w_br_c, w_out, final_g):
    nb, s, _ = x.shape
    ctx_len = ctx.shape[1]
    depth = w_in.shape[0]

    n_cond = -(-(nb + 1) // 8) * 8
    cc = jnp.zeros((n_cond, D_MODEL), F32).at[:nb].set(c).at[nb].set(c_ctx)
    mods = _mods_call(cc, w_ada, b_ada).reshape(depth, n_cond, 1, 3 * D_MODEL)

    q_cols = (jnp.arange(D_IN) >= Q_LO) & (jnp.arange(D_IN) < KV_LO)
    col_scale = jnp.where(q_cols, HEAD_DIM ** -0.5 * LOG2_E, 1.0).astype(F32)
    w_in_b = (w_in * col_scale).astype(BF16)
    b_in = b_in * col_scale
    w_a_b, w_f_b, w_c_b, w_o_b = (w.astype(BF16) for w in (w_br_a, w_br_f, w_br_c, w_out))
    tables_lat = _dft_tables(s)
    tables_ctx = _dft_tables(ctx_len)

    xc = ctx.reshape(1, nb * ctx_len, D_MODEL)
    for l in range(depth):
        last = l == depth - 1
        m = mods[l]
        bias = _nbr_bias_table(rpb[l])
        p_lat = _inproj_call(x, norm_g[l], m, 0, w_in_b[l], b_in[l], full=True)
        if last:
            kv = _inproj_call(xc, norm_g[l], m, nb, w_in_b[l][:, KV_LO:KV_HI],
                              b_in[l][KV_LO:KV_HI], full=False)
            kv_ctx, kc_blk, vc_blk = kv.reshape(nb, ctx_len, KV_HI - KV_LO), 0, 1
        else:
            p_ctx = _inproj_call(xc, norm_g[l], m, nb, w_in_b[l], b_in[l], full=True)
            kv_ctx, kc_blk, vc_blk = p_ctx.reshape(nb, ctx_len, P_COLS), P_K, P_V

        yf = _fourier_call(p_lat, s, tables_lat)
        yc = _nbr_attn_call(p_lat, kv_ctx, kc_blk, vc_blk, bias)
        x = _merge_call(x, p_lat, yf, yc, m, 0, w_conv[l], b_conv[l], w_a_b[l], w_f_b[l],
                        w_c_b[l], w_o_b[l], final_g, period=s, final=last)
        if not last:
            yf_c = _fourier_call(kv_ctx, ctx_len, tables_ctx).reshape(1, nb * ctx_len, BRANCH_W)
            yc_c = _ctx_attn_call(kv_ctx).reshape(1, nb * ctx_len, BRANCH_W)
            xc = _merge_call(xc, p_ctx, yf_c, yc_c, m, nb, w_conv[l], b_conv[l], w_a_b[l],
                             w_f_b[l], w_c_b[l], w_o_b[l], final_g, period=ctx_len, final=False)
    return x
```

```python
import functools

import numpy as np
import jax
import jax.numpy as jnp
from jax import lax
from jax.experimental import pallas as pl
from jax.experimental.pallas import tpu as pltpu

F32 = jnp.float32
BF16 = jnp.bfloat16

D_MODEL = 1024
BRANCH_W = 512
N_FGROUPS = 4
FGROUP_W = BRANCH_W // N_FGROUPS
N_HEADS = 8
HEAD_DIM = 64
HEADS_PER_PACK = 4
PACK_W = HEADS_PER_PACK * HEAD_DIM
N_PACKS = N_HEADS // HEADS_PER_PACK
GRID_W = 64
WIN_H = 8
WIN_W = 16
EPS = 1e-6
NEG_INF = -1e30
LOG2_E = float(np.log2(np.e))
D_IN = 8192
Q_LO = 3072
KV_LO = 3584
KV_HI = 4608

P_U, P_ABZ, P_FU, P_SFZ, P_Q, P_K, P_V, P_SCZ = range(8)
P_GATES = 8
P_COLS = 8 * BRANCH_W + 3 * D_MODEL

TOKEN_TILE = 512
HALO_ROWS = 16
ATTN_ROWS_PER_STEP = 8
DFT_ROW_TILE = 512
DFT_FOLD_TILE = 256
VMEM_LIMIT = 56 * 1024 * 1024


def _silu(v):
    return v * jax.nn.sigmoid(v)


def _params(n_grid_dims, vmem=VMEM_LIMIT):
    return pltpu.CompilerParams(dimension_semantics=("arbitrary",) * n_grid_dims,
                                vmem_limit_bytes=vmem)


def _mods_kernel(c_ref, w_ref, b_ref, o_ref):
    s = _silu(c_ref[...])
    o_ref[0] = jnp.dot(s, w_ref[0], preferred_element_type=F32,
                       precision=lax.Precision.HIGHEST) + b_ref[0]


def _mods_call(cc, w_ada, b_ada):
    depth = w_ada.shape[0]
    rows = cc.shape[0]
    return pl.pallas_call(
        _mods_kernel,
        grid=(depth, 3),
        in_specs=[
            pl.BlockSpec((rows, D_MODEL), lambda l, j: (0, 0)),
            pl.BlockSpec((1, D_MODEL, D_MODEL), lambda l, j: (l, 0, j)),
            pl.BlockSpec((1, 1, D_MODEL), lambda l, j: (l, 0, j)),
        ],
        out_specs=pl.BlockSpec((1, rows, D_MODEL), lambda l, j: (l, 0, j)),
        out_shape=jax.ShapeDtypeStruct((depth, rows, 3 * D_MODEL), F32),
        compiler_params=_params(2),
        name="adaln_mods",
    )(cc, w_ada, b_ada.reshape(depth, 1, 3 * D_MODEL))


def _inproj_kernel(x_ref, g_ref, sh_ref, sc_ref, w_ref, b_ref, o_ref, h_ref, *, full):
    x = x_ref[0]
    y = x * lax.rsqrt(jnp.mean(x * x, axis=-1, keepdims=True) + EPS) * g_ref[...]
    h_ref[...] = (y * (1.0 + sc_ref[0]) + sh_ref[0]).astype(BF16)

    def proj(c, n=1):
        cols = slice(c * BRANCH_W, (c + n) * BRANCH_W)
        return jnp.dot(h_ref[...], w_ref[:, cols], preferred_element_type=F32) + b_ref[:, cols]

    def put(c, v):
        o_ref[0, :, c * BRANCH_W:c * BRANCH_W + v.shape[1]] = v.astype(BF16)

    if not full:
        put(0, proj(0, 2))
        return
    for c in range(0, 6, 2):
        put(P_GATES + c, jax.nn.sigmoid(proj(10 + c, 2)))
    put(P_ABZ, proj(1) * _silu(proj(3)))
    put(P_SFZ, _silu(proj(5)))
    put(P_SCZ, _silu(proj(9)))
    put(P_U, proj(0) * proj(2))
    put(P_FU, proj(4))
    put(P_Q, proj(6, 3))


def _inproj_call(x, norm_g, mods, mod_row0, w, b, layer, w_layer, *, full):
    nb, s, _ = x.shape
    n_in = w.shape[2]
    n_out = P_COLS if full else n_in
    tm = TOKEN_TILE
    mod_spec = lambda j: pl.BlockSpec((None, 1, 1, D_MODEL),
                                      lambda bi, i: (layer, bi + mod_row0, 0, j))
    return pl.pallas_call(
        functools.partial(_inproj_kernel, full=full),
        grid=(nb, s // tm),
        in_specs=[
            pl.BlockSpec((1, tm, D_MODEL), lambda bi, i: (bi, i, 0)),
            pl.BlockSpec((None, 1, D_MODEL), lambda bi, i: (layer, 0, 0)),
            mod_spec(0),
            mod_spec(1),
            pl.BlockSpec((None, D_MODEL, n_in), lambda bi, i: (w_layer, 0, 0),
                         pipeline_mode=pl.Buffered(1)),
            pl.BlockSpec((None, 1, n_in), lambda bi, i: (w_layer, 0, 0)),
        ],
        out_specs=pl.BlockSpec((1, tm, n_out), lambda bi, i: (bi, i, 0)),
        out_shape=jax.ShapeDtypeStruct((nb, s, n_out), BF16),
        scratch_shapes=[pltpu.VMEM((tm, D_MODEL), BF16)],
        compiler_params=_params(2),
        name="inproj_full" if full else "inproj_kv",
    )(x, norm_g, mods, mods, w, b)


def _dft_tables(seq_len):
    half = seq_len // 2
    ck = lax.broadcasted_iota(jnp.int32, (FGROUP_W, FGROUP_W), 0)
    cm = lax.broadcasted_iota(jnp.int32, (FGROUP_W, FGROUP_W), 1)
    ac = ((ck * cm) % FGROUP_W).astype(F32) * (2.0 * np.pi / FGROUP_W)
    zero = jnp.zeros_like(ac)
    chan = jnp.concatenate([jnp.concatenate([jnp.cos(ac), zero], axis=1),
                            jnp.concatenate([zero, jnp.sin(ac)], axis=1)], axis=0)
    chan = chan * (FGROUP_W ** -0.5)

    blk = min(64, half)
    a = lax.broadcasted_iota(jnp.int32, (half // blk, 1, half), 0)
    b = lax.broadcasted_iota(jnp.int32, (1, blk, half), 1)
    unit = 2.0 * np.pi / seq_len
    ang_a = ((a * blk * lax.broadcasted_iota(jnp.int32, a.shape, 2)) % seq_len).astype(F32) * unit
    ang_b = ((b * lax.broadcasted_iota(jnp.int32, b.shape, 2)) % seq_len).astype(F32) * unit
    ca, sa, cb, sb = jnp.cos(ang_a), jnp.sin(ang_a), jnp.cos(ang_b), jnp.sin(ang_b)
    cos_t = (ca * cb - sa * sb).reshape(half, half)
    sin_t = (sa * cb + ca * sb).reshape(half, half)
    k = lax.broadcasted_iota(jnp.int32, (half, half), 0)
    l = lax.broadcasted_iota(jnp.int32, (half, half), 1)
    nyquist = jnp.where(k % 2 == 0, 1.0, -1.0)
    top = jnp.concatenate([cos_t, jnp.where(l == 0, nyquist, -sin_t)], axis=1)
    col_sign = jnp.where(lax.broadcasted_iota(jnp.int32, (1, seq_len), 1) % 2 == 0, 1.0, -1.0)
    pos = jnp.concatenate([top, top * col_sign], axis=0) * (seq_len ** -0.5)
    return chan.astype(BF16), pos.astype(BF16)


def _fourier_kernel(fu_ref, fz_ref, chan_ref, pos_ref, o_ref, v_ref, *, seq_len, tile):
    half = seq_len // 2

    @pl.when(pl.program_id(1) == 0)
    def _():
        flip = (lax.broadcasted_iota(jnp.int32, (tile, tile), 0)
                + lax.broadcasted_iota(jnp.int32, (tile, tile), 1) == tile).astype(BF16)
        row0 = lax.broadcasted_iota(jnp.int32, (tile, 1), 0) == 0
        for t in range(half // tile):
            lo = fu_ref[0, t * tile:(t + 1) * tile, :].astype(F32)
            hi0 = seq_len - (t + 1) * tile
            rev = jnp.dot(flip, fu_ref[0, hi0:hi0 + tile, :], preferred_element_type=F32)
            if t > 0:
                nxt = fu_ref[0, hi0 + tile:hi0 + tile + HALO_ROWS, :].astype(F32)
                rev = jnp.where(row0, nxt[0:1, :], rev)
            u_plus = (lo + rev).astype(BF16)
            u_minus = (lo - rev).astype(BF16)
            for g in range(N_FGROUPS):
                cols = slice(g * FGROUP_W, (g + 1) * FGROUP_W)
                ab = jnp.dot(jnp.concatenate([u_plus[:, cols], u_minus[:, cols]], axis=1),
                             chan_ref[...], preferred_element_type=F32)
                a, b = ab[:, :FGROUP_W], ab[:, FGROUP_W:]
                if t == 0:
                    u_mid = fu_ref[0, half:half + HALO_ROWS, cols]
                    a_mid = jnp.dot(jnp.concatenate([u_mid, jnp.zeros_like(u_mid)], axis=1),
                                    chan_ref[...], preferred_element_type=F32)
                    b = jnp.where(row0, a_mid[0:1, :FGROUP_W], b)
                v_ref[t * tile:(t + 1) * tile, cols] = a.astype(BF16)
                v_ref[half + t * tile:half + (t + 1) * tile, cols] = b.astype(BF16)

    y = jnp.dot(pos_ref[...], v_ref[...], preferred_element_type=F32)
    o_ref[0] = (y * fz_ref[0].astype(F32)).astype(BF16)


def _fourier_call(p, seq_len, tables):
    n_seq = p.shape[0]
    chan, pos = tables
    tk = min(DFT_ROW_TILE, seq_len)
    return pl.pallas_call(
        functools.partial(_fourier_kernel, seq_len=seq_len, tile=min(DFT_FOLD_TILE, seq_len // 2)),
        grid=(n_seq, seq_len // tk),
        in_specs=[
            pl.BlockSpec((1, seq_len, BRANCH_W), lambda b, j: (b, 0, P_FU)),
            pl.BlockSpec((1, tk, BRANCH_W), lambda b, j: (b, j, P_SFZ)),
            pl.BlockSpec((2 * FGROUP_W, 2 * FGROUP_W), lambda b, j: (0, 0)),
            pl.BlockSpec((tk, seq_len), lambda b, j: (j, 0)),
        ],
        out_specs=pl.BlockSpec((1, tk, BRANCH_W), lambda b, j: (b, j, 0)),
        out_shape=jax.ShapeDtypeStruct((n_seq, seq_len, BRANCH_W), BF16),
        scratch_shapes=[pltpu.VMEM((seq_len, BRANCH_W), BF16)],
        compiler_params=_params(2),
        name=f"fourier_{seq_len}",
    )(p, p, chan, pos)


def _lane_head():
    return lax.broadcasted_iota(jnp.int32, (1, PACK_W), 1) // HEAD_DIM


def _stack_heads(q_pack):
    lane_head = _lane_head()
    zero = jnp.zeros_like(q_pack)
    return jnp.concatenate([jnp.where(lane_head == h, q_pack, zero)
                            for h in range(HEADS_PER_PACK)], axis=0)


def _unstack_heads(o_stack, n):
    lane_head = _lane_head()
    out = jnp.zeros((n, PACK_W), F32)
    for h in range(HEADS_PER_PACK):
        out = jnp.where(lane_head == h, o_stack[h * n:(h + 1) * n], out)
    return out


def _nt_dot(a, b):
    return lax.dot_general(a, b, (((1,), (1,)), ((), ())), preferred_element_type=F32)


def _row_reduce(blocks, combine, reduce):
    chunks = [b[:, i:i + 128] for b in blocks for i in range(0, b.shape[1], 128)]
    return reduce(functools.reduce(combine, chunks), axis=-1, keepdims=True)


def _softmax_weights(scores):
    m = _row_reduce(scores, jnp.maximum, jnp.max)
    es = [jnp.exp2(s - m) for s in scores]
    return [e.astype(BF16) for e in es], _row_reduce(es, jnp.add, jnp.sum)


def _nbr_attn_kernel(q_ref, k_ref, v_ref, kc_ref, vc_ref, bias_ref, z_ref, o_ref, *, n_rows):
    step = pl.program_id(1)
    win = WIN_H * GRID_W
    for j in range(ATTN_ROWS_PER_STEP):
        r = step * ATTN_ROWS_PER_STEP + j
        rs = jnp.clip(r - WIN_H // 2, 0, n_rows - WIN_H)
        k0 = pl.multiple_of(rs * GRID_W, GRID_W)
        q_rows = slice(j * GRID_W, (j + 1) * GRID_W)
        for pk in range(N_PACKS):
            cols = slice(pk * PACK_W, (pk + 1) * PACK_W)
            qm = _stack_heads(q_ref[0, q_rows, cols])
            s_loc = _nt_dot(qm, k_ref[0, pl.ds(k0, win), cols]) + bias_ref[r - rs, pk]
            s_ctx = _nt_dot(qm, kc_ref[0, :, cols])
            (p_loc, p_ctx), denom = _softmax_weights([s_loc, s_ctx])
            o = (jnp.dot(p_loc, v_ref[0, pl.ds(k0, win), cols], preferred_element_type=F32)
                 + jnp.dot(p_ctx, vc_ref[0, :, cols], preferred_element_type=F32))
            o = _unstack_heads(o / denom, GRID_W)
            z = z_ref[0, q_rows, cols].astype(F32)
            o_ref[0, q_rows, cols] = (o * z).astype(BF16)


def _nbr_bias_table(rpb):
    depth = rpb.shape[0]
    qcol = np.arange(GRID_W)[:, None]
    kcol = np.arange(GRID_W)[None, :]
    cs = np.clip(qcol - WIN_W // 2, 0, GRID_W - WIN_W)
    col_ok = (kcol >= cs) & (kcol < cs + WIN_W)
    dc_idx = np.clip(kcol - qcol + WIN_W - 1, 0, 2 * WIN_W - 2)
    rpb_col = rpb[..., dc_idx]
    per_delta = jnp.stack([rpb_col[:, :, WIN_H - 1 - d:2 * WIN_H - 1 - d] for d in range(WIN_H)])
    t = per_delta.transpose(1, 0, 2, 4, 3, 5)
    t = jnp.where(col_ok[:, None, :], t * LOG2_E, NEG_INF)
    return t.reshape(depth, WIN_H, N_PACKS, HEADS_PER_PACK * GRID_W, WIN_H * GRID_W).astype(F32)


def _nbr_attn_call(p, kv_ctx, kc_blk, vc_blk, bias, layer):
    nb, s, _ = p.shape
    ctx_len = kv_ctx.shape[1]
    n_rows = s // GRID_W
    tq = ATTN_ROWS_PER_STEP * GRID_W
    return pl.pallas_call(
        functools.partial(_nbr_attn_kernel, n_rows=n_rows),
        grid=(nb, n_rows // ATTN_ROWS_PER_STEP),
        in_specs=[
            pl.BlockSpec((1, tq, BRANCH_W), lambda b, i: (b, i, P_Q)),
            pl.BlockSpec((1, s, BRANCH_W), lambda b, i: (b, 0, P_K)),
            pl.BlockSpec((1, s, BRANCH_W), lambda b, i: (b, 0, P_V)),
            pl.BlockSpec((1, ctx_len, BRANCH_W), lambda b, i: (b, 0, kc_blk)),
            pl.BlockSpec((1, ctx_len, BRANCH_W), lambda b, i: (b, 0, vc_blk)),
            pl.BlockSpec((None,) + bias.shape[1:], lambda b, i: (layer, 0, 0, 0, 0),
                         pipeline_mode=pl.Buffered(1)),
            pl.BlockSpec((1, tq, BRANCH_W), lambda b, i: (b, i, P_SCZ)),
        ],
        out_specs=pl.BlockSpec((1, tq, BRANCH_W), lambda b, i: (b, i, 0)),
        out_shape=jax.ShapeDtypeStruct((nb, s, BRANCH_W), BF16),
        compiler_params=_params(2),
        name="nbr_attention",
    )(p, p, p, kv_ctx, kv_ctx, bias, p)


def _ctx_attn_kernel(q_ref, k_ref, v_ref, z_ref, o_ref, *, ctx_len):
    for q0 in range(0, ctx_len, GRID_W):
        for pk in range(N_PACKS):
            cols = slice(pk * PACK_W, (pk + 1) * PACK_W)
            qm = _stack_heads(q_ref[0, q0:q0 + GRID_W, cols])
            (p,), d = _softmax_weights([_nt_dot(qm, k_ref[0, :, cols])])
            o = jnp.dot(p, v_ref[0, :, cols], preferred_element_type=F32) / d
            o = _unstack_heads(o, GRID_W)
            z = z_ref[0, q0:q0 + GRID_W, cols].astype(F32)
            o_ref[0, q0:q0 + GRID_W, cols] = (o * z).astype(BF16)


def _ctx_attn_call(pc):
    nb, ctx_len, _ = pc.shape
    blk = lambda c: pl.BlockSpec((1, ctx_len, BRANCH_W), lambda b: (b, 0, c))
    return pl.pallas_call(
        functools.partial(_ctx_attn_kernel, ctx_len=ctx_len),
        grid=(nb,),
        in_specs=[blk(P_Q), blk(P_K), blk(P_V), blk(P_SCZ)],
        out_specs=blk(0),
        out_shape=jax.ShapeDtypeStruct((nb, ctx_len, BRANCH_W), BF16),
        compiler_params=_params(1),
        name="ctx_attention",
    )(pc, pc, pc, pc)


def _merge_kernel(x_ref, u_ref, up_ref, un_ref, abz_ref, yf_ref, yc_ref, ga_ref, gf_ref, gc_ref,
                  gt_ref, wcv_ref, bcv_ref, wa_ref, wf_ref, wc_ref, wo_ref, fg_ref, o_ref,
                  *, period, final):
    tm = u_ref.shape[1]
    u = u_ref[0].astype(F32)
    row = lax.broadcasted_iota(jnp.int32, (tm, 1), 0)
    pos = (pl.program_id(1) * tm + row) % period
    u_prev = pltpu.roll(u, 1, 0)
    u_prev = jnp.where(row == 0, up_ref[0, HALO_ROWS - 1:HALO_ROWS, :].astype(F32), u_prev)
    u_prev = jnp.where(pos == 0, 0.0, u_prev)
    u_next = pltpu.roll(u, tm - 1, 0)
    u_next = jnp.where(row == tm - 1, un_ref[0, 0:1, :].astype(F32), u_next)
    u_next = jnp.where(pos == period - 1, 0.0, u_next)
    conv = u_prev * wcv_ref[0:1, :] + u * wcv_ref[1:2, :] + u_next * wcv_ref[2:3, :] + bcv_ref[...]
    ya_in = (abz_ref[0].astype(F32) * conv).astype(BF16)

    y_a = jnp.dot(ya_in, wa_ref[...], preferred_element_type=F32)
    y_f = jnp.dot(yf_ref[0], wf_ref[...], preferred_element_type=F32)
    y_c = jnp.dot(yc_ref[0], wc_ref[...], preferred_element_type=F32)
    mix = (ga_ref[0].astype(F32) * y_a + gf_ref[0].astype(F32) * y_f
           + gc_ref[0].astype(F32) * y_c)
    z = jnp.dot(mix.astype(BF16), wo_ref[...], preferred_element_type=F32)
    xn = x_ref[0] + gt_ref[0] * z
    if final:
        xn = xn * lax.rsqrt(jnp.mean(xn * xn, axis=-1, keepdims=True) + EPS) * fg_ref[...]
    o_ref[0] = xn


def _merge_call(x, p, yf, yc, mods, mod_row0, w_conv, b_conv, w_a, w_f, w_c, w_o, final_g, layer,
                *, period, final):
    nb, s, _ = x.shape
    tm = TOKEN_TILE
    halo_per_tile = tm // HALO_ROWS
    n_halo = s // HALO_ROWS
    tok = lambda width, c: pl.BlockSpec((1, tm, width), lambda b, i: (b, i, c))
    param = lambda a: pl.BlockSpec((None,) + a.shape[1:], lambda b, i: (layer, 0, 0))
    return pl.pallas_call(
        functools.partial(_merge_kernel, period=period, final=final),
        grid=(nb, s // tm),
        in_specs=[
            tok(D_MODEL, 0),
            tok(BRANCH_W, P_U),
            pl.BlockSpec((1, HALO_ROWS, BRANCH_W),
                         lambda b, i: (b, jnp.maximum(i * halo_per_tile - 1, 0), P_U)),
            pl.BlockSpec((1, HALO_ROWS, BRANCH_W),
                         lambda b, i: (b, jnp.minimum((i + 1) * halo_per_tile, n_halo - 1), P_U)),
            tok(BRANCH_W, P_ABZ),
            tok(BRANCH_W, 0),
            tok(BRANCH_W, 0),
            tok(D_MODEL, P_GATES // 2),
            tok(D_MODEL, P_GATES // 2 + 1),
            tok(D_MODEL, P_GATES // 2 + 2),
            pl.BlockSpec((None, 1, 1, D_MODEL), lambda b, i: (layer, b + mod_row0, 0, 2)),
            param(w_conv),
            param(b_conv),
            param(w_a),
            param(w_f),
            param(w_c),
            param(w_o),
            pl.BlockSpec((1, D_MODEL), lambda b, i: (0, 0)),
        ],
        out_specs=tok(D_MODEL, 0),
        out_shape=jax.ShapeDtypeStruct((nb, s, D_MODEL), F32),
        compiler_params=_params(2),
        name="merge_final" if final else "merge",
    )(x, p, p, p, p, yf, yc, p, p, p, mods, w_conv, b_conv, w_a, w_f, w_c, w_o, final_g)


def kernel(x, c, ctx, c_ctx, norm_g, w_ada, b_ada, w_in, b_in, w_conv, b_conv, rpb,
           w_br_a, w_br_f, w_br_c, w_out, final_g):
    nb, s, _ = x.shape
    ctx_len = ctx.shape[1]
    depth = w_in.shape[0]

    n_cond = -(-(nb + 1) // 8) * 8
    cc = jnp.zeros((n_cond, D_MODEL), F32).at[:nb].set(c).at[nb].set(c_ctx)
    mods = _mods_call(cc, w_ada, b_ada).reshape(depth, n_cond, 1, 3 * D_MODEL)

    q_cols = (jnp.arange(D_IN) >= Q_LO) & (jnp.arange(D_IN) < KV_LO)
    col_scale = jnp.where(q_cols, HEAD_DIM ** -0.5 * LOG2_E, 1.0).astype(F32)
    w_in_b = (w_in * col_scale).astype(BF16)
    b_in = (b_in * col_scale).reshape(depth, 1, D_IN)
    w_kv = w_in_b[depth - 1:, :, KV_LO:KV_HI]
    b_kv = b_in[depth - 1:, :, KV_LO:KV_HI]
    w_a, w_f, w_c, w_o = (w.astype(BF16) for w in (w_br_a, w_br_f, w_br_c, w_out))
    norm_g = norm_g.reshape(depth, 1, D_MODEL)
    b_conv = b_conv.reshape(depth, 1, BRANCH_W)
    final_g = final_g.reshape(1, D_MODEL)
    bias = _nbr_bias_table(rpb)
    tables_lat = _dft_tables(s)
    tables_ctx = _dft_tables(ctx_len)

    xc = ctx.reshape(1, nb * ctx_len, D_MODEL)
    for l in range(depth):
        last = l == depth - 1
        p_lat = _inproj_call(x, norm_g, mods, 0, w_in_b, b_in, l, l, full=True)
        if last:
            kv = _inproj_call(xc, norm_g, mods, nb, w_kv, b_kv, l, 0, full=False)
            kv_ctx, kc_blk, vc_blk = kv.reshape(nb, ctx_len, KV_HI - KV_LO), 0, 1
        else:
            p_ctx = _inproj_call(xc, norm_g, mods, nb, w_in_b, b_in, l, l, full=True)
            kv_ctx, kc_blk, vc_blk = p_ctx.reshape(nb, ctx_len, P_COLS), P_K, P_V

        yf = _fourier_call(p_lat, s, tables_lat)
        yc = _nbr_attn_call(p_lat, kv_ctx, kc_blk, vc_blk, bias, l)
        x = _merge_call(x, p_lat, yf, yc, mods, 0, w_conv, b_conv, w_a, w_f, w_c, w_o, final_g, l,
                        period=s, final=last)
        if not last:
            yf_c = _fourier_call(kv_ctx, ctx_len, tables_ctx).reshape(1, nb * ctx_len, BRANCH_W)
            yc_c = _ctx_attn_call(kv_ctx).reshape(1, nb * ctx_len, BRANCH_W)
            xc = _merge_call(xc, p_ctx, yf_c, yc_c, mods, nb, w_conv, b_conv, w_a, w_f, w_c, w_o,
                             final_g, l, period=ctx_len, final=False)
    return x
```

```python
import functools

import numpy as np
import jax
import jax.numpy as jnp
from jax import lax
from jax.experimental import pallas as pl
from jax.experimental.pallas import tpu as pltpu

F32 = jnp.float32
BF16 = jnp.bfloat16

D_MODEL = 1024
BRANCH_W = 512
N_FGROUPS = 4
FGROUP_W = BRANCH_W // N_FGROUPS
N_HEADS = 8
HEAD_DIM = 64
HEADS_PER_PACK = 4
PACK_W = HEADS_PER_PACK * HEAD_DIM
N_PACKS = N_HEADS // HEADS_PER_PACK
GRID_W = 64
WIN_H = 8
WIN_W = 16
EPS = 1e-6
NEG_INF = -1e30
LOG2_E = float(np.log2(np.e))
D_IN = 8192
Q_LO = 3072
KV_LO = 3584
KV_HI = 4608

P_U, P_ABZ, P_FU, P_SFZ, P_Q, P_K, P_V, P_SCZ = range(8)
P_GATES = 8
P_COLS = 8 * BRANCH_W + 3 * D_MODEL

TOKEN_TILE = 512
HALO_ROWS = 16
ATTN_ROWS_PER_STEP = 8
DFT_FOLD_TILE = 256
VMEM_LIMIT = 56 * 1024 * 1024


def _silu(v):
    return v * jax.nn.sigmoid(v)


def _params(n_grid_dims, vmem=VMEM_LIMIT):
    return pltpu.CompilerParams(dimension_semantics=("arbitrary",) * n_grid_dims,
                                vmem_limit_bytes=vmem)


def _mods_kernel(c_ref, w_ref, b_ref, o_ref):
    s = _silu(c_ref[...])
    o_ref[0] = jnp.dot(s, w_ref[0], preferred_element_type=F32,
                       precision=lax.Precision.HIGHEST) + b_ref[0]


def _mods_call(cc, w_ada, b_ada):
    depth = w_ada.shape[0]
    rows = cc.shape[0]
    return pl.pallas_call(
        _mods_kernel,
        grid=(depth, 3),
        in_specs=[
            pl.BlockSpec((rows, D_MODEL), lambda l, j: (0, 0)),
            pl.BlockSpec((1, D_MODEL, D_MODEL), lambda l, j: (l, 0, j)),
            pl.BlockSpec((1, 1, D_MODEL), lambda l, j: (l, 0, j)),
        ],
        out_specs=pl.BlockSpec((1, rows, D_MODEL), lambda l, j: (l, 0, j)),
        out_shape=jax.ShapeDtypeStruct((depth, rows, 3 * D_MODEL), F32),
        compiler_params=_params(2),
        name="adaln_mods",
    )(cc, w_ada, b_ada.reshape(depth, 1, 3 * D_MODEL))


def _inproj_kernel(x_ref, g_ref, sh_ref, sc_ref, w_ref, b_ref, o_ref, h_ref, *, full):
    x = x_ref[0]
    y = x * lax.rsqrt(jnp.mean(x * x, axis=-1, keepdims=True) + EPS) * g_ref[...]
    h_ref[...] = (y * (1.0 + sc_ref[0]) + sh_ref[0]).astype(BF16)

    def proj(c, n=1):
        cols = slice(c * BRANCH_W, (c + n) * BRANCH_W)
        return jnp.dot(h_ref[...], w_ref[:, cols], preferred_element_type=F32) + b_ref[:, cols]

    def put(c, v):
        o_ref[0, :, c * BRANCH_W:c * BRANCH_W + v.shape[1]] = v.astype(BF16)

    if not full:
        put(0, proj(0, 2))
        return
    for c in range(0, 6, 2):
        put(P_GATES + c, jax.nn.sigmoid(proj(10 + c, 2)))
    put(P_ABZ, proj(1) * _silu(proj(3)))
    put(P_SFZ, _silu(proj(5)))
    put(P_SCZ, _silu(proj(9)))
    put(P_U, proj(0) * proj(2))
    put(P_FU, proj(4))
    put(P_Q, proj(6, 3))


def _inproj_call(x, norm_g, mods, mod_row0, w, b, layer, w_layer, *, full):
    nb, s, _ = x.shape
    n_in = w.shape[2]
    n_out = P_COLS if full else n_in
    tm = TOKEN_TILE
    mod_spec = lambda j: pl.BlockSpec((None, 1, 1, D_MODEL),
                                      lambda bi, i: (layer, bi + mod_row0, 0, j))
    return pl.pallas_call(
        functools.partial(_inproj_kernel, full=full),
        grid=(nb, s // tm),
        in_specs=[
            pl.BlockSpec((1, tm, D_MODEL), lambda bi, i: (bi, i, 0)),
            pl.BlockSpec((None, 1, D_MODEL), lambda bi, i: (layer, 0, 0)),
            mod_spec(0),
            mod_spec(1),
            pl.BlockSpec((None, D_MODEL, n_in), lambda bi, i: (w_layer, 0, 0),
                         pipeline_mode=pl.Buffered(1)),
            pl.BlockSpec((None, 1, n_in), lambda bi, i: (w_layer, 0, 0)),
        ],
        out_specs=pl.BlockSpec((1, tm, n_out), lambda bi, i: (bi, i, 0)),
        out_shape=jax.ShapeDtypeStruct((nb, s, n_out), BF16),
        scratch_shapes=[pltpu.VMEM((tm, D_MODEL), BF16)],
        compiler_params=_params(2),
        name="inproj_full" if full else "inproj_kv",
    )(x, norm_g, mods, mods, w, b)


def _dft_tables(seq_len):
    half = seq_len // 2
    ck = lax.broadcasted_iota(jnp.int32, (FGROUP_W, FGROUP_W), 0)
    cm = lax.broadcasted_iota(jnp.int32, (FGROUP_W, FGROUP_W), 1)
    ac = ((ck * cm) % FGROUP_W).astype(F32) * (2.0 * np.pi / FGROUP_W)
    zero = jnp.zeros_like(ac)
    chan = jnp.concatenate([jnp.concatenate([jnp.cos(ac), zero], axis=1),
                            jnp.concatenate([zero, jnp.sin(ac)], axis=1)], axis=0)
    chan = chan * (FGROUP_W ** -0.5)

    blk = min(64, half)
    a = lax.broadcasted_iota(jnp.int32, (half // blk, 1, half), 0)
    b = lax.broadcasted_iota(jnp.int32, (1, blk, half), 1)
    unit = 2.0 * np.pi / seq_len
    ang_a = ((a * blk * lax.broadcasted_iota(jnp.int32, a.shape, 2)) % seq_len).astype(F32) * unit
    ang_b = ((b * lax.broadcasted_iota(jnp.int32, b.shape, 2)) % seq_len).astype(F32) * unit
    ca, sa, cb, sb = jnp.cos(ang_a), jnp.sin(ang_a), jnp.cos(ang_b), jnp.sin(ang_b)
    cos_t = (ca * cb - sa * sb).reshape(half, half)
    sin_t = (sa * cb + ca * sb).reshape(half, half)
    k = lax.broadcasted_iota(jnp.int32, (half, half), 0)
    l = lax.broadcasted_iota(jnp.int32, (half, half), 1)
    nyquist = jnp.where(k % 2 == 0, 1.0, -1.0)
    pos = jnp.concatenate([cos_t, jnp.where(l == 0, nyquist, -sin_t)], axis=1) * (seq_len ** -0.5)
    return chan.astype(BF16), pos.astype(BF16)


def _fourier_kernel(fu_ref, fz_top_ref, fz_bot_ref, chan_ref, pos_ref, o_ref, v_ref, nyq_ref,
                    carry_ref, *, seq_len, tile, ft):
    half = seq_len // 2
    n_tiles = half // tile
    step = pl.program_id(1)
    flip = (lax.broadcasted_iota(jnp.int32, (ft, ft), 0)
            + lax.broadcasted_iota(jnp.int32, (ft, ft), 1) == ft).astype(BF16)
    row0 = lax.broadcasted_iota(jnp.int32, (ft, 1), 0) == 0

    @pl.when(step == 0)
    def _():
        for t in range(half // ft):
            lo = fu_ref[0, t * ft:(t + 1) * ft, :].astype(F32)
            hi0 = seq_len - (t + 1) * ft
            rev = jnp.dot(flip, fu_ref[0, hi0:hi0 + ft, :], preferred_element_type=F32)
            if t > 0:
                nxt = fu_ref[0, hi0 + ft:hi0 + ft + HALO_ROWS, :].astype(F32)
                rev = jnp.where(row0, nxt[0:1, :], rev)
            u_plus = (lo + rev).astype(BF16)
            u_minus = (lo - rev).astype(BF16)
            for g in range(N_FGROUPS):
                cols = slice(g * FGROUP_W, (g + 1) * FGROUP_W)
                ab = jnp.dot(jnp.concatenate([u_plus[:, cols], u_minus[:, cols]], axis=1),
                             chan_ref[...], preferred_element_type=F32)
                a, b = ab[:, :FGROUP_W], ab[:, FGROUP_W:]
                if t == 0:
                    u_mid = fu_ref[0, half:half + HALO_ROWS, cols]
                    a_mid = jnp.dot(jnp.concatenate([u_mid, jnp.zeros_like(u_mid)], axis=1),
                                    chan_ref[...], preferred_element_type=F32)
                    a_mid = a_mid[0:8, :FGROUP_W].astype(BF16).astype(F32)
                    b = jnp.where(row0, a_mid[0:1, :], b)
                    nyq_ref[:, cols] = a_mid * (2.0 * seq_len ** -0.5)
                v_ref[t * ft:(t + 1) * ft, cols] = a.astype(BF16)
                v_ref[half + t * ft:half + (t + 1) * ft, cols] = b.astype(BF16)
        sign = jnp.where(lax.broadcasted_iota(jnp.int32, (HALO_ROWS, half), 1) % 2 == 0,
                         seq_len ** -0.5, -(seq_len ** -0.5)).astype(BF16)
        y_mid = jnp.dot(sign, v_ref[0:half, :], preferred_element_type=F32)
        carry_ref[...] = y_mid[0:8, :] + 0.5 * nyq_ref[...]

    t = n_tiles - 1 - step
    r0 = pl.multiple_of(t * tile, tile)
    a_k = jnp.dot(pos_ref[pl.ds(r0, tile), :half], v_ref[:half, :], preferred_element_type=F32)
    b_k = jnp.dot(pos_ref[pl.ds(r0, tile), half:], v_ref[half:, :], preferred_element_type=F32)
    o_ref[0, 0, 0] = ((a_k + b_k) * fz_top_ref[0].astype(F32)).astype(BF16)

    k_even = lax.broadcasted_iota(jnp.int32, (tile, 1), 0) % 2 == 0
    nyq = nyq_ref[0:1, :]
    x = a_k - b_k + jnp.where(k_even, nyq, -nyq)
    x_hi = x.astype(BF16)
    x_lo = (x - x_hi.astype(F32)).astype(BF16)
    n_sub = tile // ft
    pieces = []
    for jp in range(n_sub):
        j = n_sub - 1 - jp
        rows = slice(j * ft, (j + 1) * ft)
        rev = (jnp.dot(flip, x_hi[rows], preferred_element_type=F32)
               + jnp.dot(flip, x_lo[rows], preferred_element_type=F32))
        first = carry_ref[0:1, :] if jp == 0 else x[(j + 1) * ft:(j + 1) * ft + 1, :]
        pieces.append(jnp.where(row0, first, rev))
    bottom = jnp.concatenate(pieces, axis=0)
    o_ref[0, 0, 1] = (bottom * fz_bot_ref[0].astype(F32)).astype(BF16)
    carry_ref[...] = x[0:8, :]


def _fourier_call(p, seq_len, tables):
    n_seq = p.shape[0]
    chan, pos = tables
    half = seq_len // 2
    tile = min(TOKEN_TILE, half)
    n_tiles = half // tile
    return pl.pallas_call(
        functools.partial(_fourier_kernel, seq_len=seq_len, tile=tile, ft=min(DFT_FOLD_TILE, tile)),
        grid=(n_seq, n_tiles),
        in_specs=[
            pl.BlockSpec((1, seq_len, BRANCH_W), lambda b, s: (b, 0, P_FU)),
            pl.BlockSpec((1, tile, BRANCH_W), lambda b, s: (b, n_tiles - 1 - s, P_SFZ)),
            pl.BlockSpec((1, tile, BRANCH_W), lambda b, s: (b, n_tiles + s, P_SFZ)),
            pl.BlockSpec((2 * FGROUP_W, 2 * FGROUP_W), lambda b, s: (0, 0)),
            pl.BlockSpec((half, seq_len), lambda b, s: (0, 0), pipeline_mode=pl.Buffered(1)),
        ],
        out_specs=pl.BlockSpec((1, 1, 2, tile, BRANCH_W), lambda b, s: (b, s, 0, 0, 0)),
        out_shape=jax.ShapeDtypeStruct((n_seq, n_tiles, 2, tile, BRANCH_W), BF16),
        scratch_shapes=[pltpu.VMEM((seq_len, BRANCH_W), BF16),
                        pltpu.VMEM((8, BRANCH_W), F32),
                        pltpu.VMEM((8, BRANCH_W), F32)],
        compiler_params=_params(2),
        name=f"fourier_{seq_len}",
    )(p, p, p, chan, pos)


def _lane_head():
    return lax.broadcasted_iota(jnp.int32, (1, PACK_W), 1) // HEAD_DIM


def _stack_heads(q_pack):
    lane_head = _lane_head()
    zero = jnp.zeros_like(q_pack)
    return jnp.concatenate([jnp.where(lane_head == h, q_pack, zero)
                            for h in range(HEADS_PER_PACK)], axis=0)


def _unstack_heads(o_stack, n):
    lane_head = _lane_head()
    out = jnp.zeros((n, PACK_W), F32)
    for h in range(HEADS_PER_PACK):
        out = jnp.where(lane_head == h, o_stack[h * n:(h + 1) * n], out)
    return out


def _nt_dot(a, b):
    return lax.dot_general(a, b, (((1,), (1,)), ((), ())), preferred_element_type=F32)


def _row_reduce(blocks, combine, reduce):
    chunks = [b[:, i:i + 128] for b in blocks for i in range(0, b.shape[1], 128)]
    return reduce(functools.reduce(combine, chunks), axis=-1, keepdims=True)


def _softmax_weights(scores):
    m = _row_reduce(scores, jnp.maximum, jnp.max)
    es = [jnp.exp2(s - m) for s in scores]
    return [e.astype(BF16) for e in es], _row_reduce(es, jnp.add, jnp.sum)


def _nbr_attn_kernel(q_ref, k_ref, v_ref, kc_ref, vc_ref, bias_ref, z_ref, o_ref, *, n_rows):
    step = pl.program_id(1)
    win = WIN_H * GRID_W
    for j in range(ATTN_ROWS_PER_STEP):
        r = step * ATTN_ROWS_PER_STEP + j
        rs = jnp.clip(r - WIN_H // 2, 0, n_rows - WIN_H)
        k0 = pl.multiple_of(rs * GRID_W, GRID_W)
        q_rows = slice(j * GRID_W, (j + 1) * GRID_W)
        for pk in range(N_PACKS):
            cols = slice(pk * PACK_W, (pk + 1) * PACK_W)
            qm = _stack_heads(q_ref[0, q_rows, cols])
            s_loc = _nt_dot(qm, k_ref[0, pl.ds(k0, win), cols]) + bias_ref[r - rs, pk]
            s_ctx = _nt_dot(qm, kc_ref[0, :, cols])
            (p_loc, p_ctx), denom = _softmax_weights([s_loc, s_ctx])
            o = (jnp.dot(p_loc, v_ref[0, pl.ds(k0, win), cols], preferred_element_type=F32)
                 + jnp.dot(p_ctx, vc_ref[0, :, cols], preferred_element_type=F32))
            o = _unstack_heads(o / denom, GRID_W)
            z = z_ref[0, q_rows, cols].astype(F32)
            o_ref[0, q_rows, cols] = (o * z).astype(BF16)


def _nbr_bias_table(rpb):
    depth = rpb.shape[0]
    qcol = np.arange(GRID_W)[:, None]
    kcol = np.arange(GRID_W)[None, :]
    cs = np.clip(qcol - WIN_W // 2, 0, GRID_W - WIN_W)
    col_ok = (kcol >= cs) & (kcol < cs + WIN_W)
    dc_idx = np.clip(kcol - qcol + WIN_W - 1, 0, 2 * WIN_W - 2)
    rpb_col = rpb[..., dc_idx]
    per_delta = jnp.stack([rpb_col[:, :, WIN_H - 1 - d:2 * WIN_H - 1 - d] for d in range(WIN_H)])
    t = per_delta.transpose(1, 0, 2, 4, 3, 5)
    t = jnp.where(col_ok[:, None, :], t * LOG2_E, NEG_INF)
    return t.reshape(depth, WIN_H, N_PACKS, HEADS_PER_PACK * GRID_W, WIN_H * GRID_W).astype(F32)


def _nbr_attn_call(p, kv_ctx, kc_blk, vc_blk, bias, layer):
    nb, s, _ = p.shape
    ctx_len = kv_ctx.shape[1]
    n_rows = s // GRID_W
    tq = ATTN_ROWS_PER_STEP * GRID_W
    return pl.pallas_call(
        functools.partial(_nbr_attn_kernel, n_rows=n_rows),
        grid=(nb, n_rows // ATTN_ROWS_PER_STEP),
        in_specs=[
            pl.BlockSpec((1, tq, BRANCH_W), lambda b, i: (b, i, P_Q)),
            pl.BlockSpec((1, s, BRANCH_W), lambda b, i: (b, 0, P_K)),
            pl.BlockSpec((1, s, BRANCH_W), lambda b, i: (b, 0, P_V)),
            pl.BlockSpec((1, ctx_len, BRANCH_W), lambda b, i: (b, 0, kc_blk)),
            pl.BlockSpec((1, ctx_len, BRANCH_W), lambda b, i: (b, 0, vc_blk)),
            pl.BlockSpec((None,) + bias.shape[1:], lambda b, i: (layer, 0, 0, 0, 0),
                         pipeline_mode=pl.Buffered(1)),
            pl.BlockSpec((1, tq, BRANCH_W), lambda b, i: (b, i, P_SCZ)),
        ],
        out_specs=pl.BlockSpec((1, tq, BRANCH_W), lambda b, i: (b, i, 0)),
        out_shape=jax.ShapeDtypeStruct((nb, s, BRANCH_W), BF16),
        compiler_params=_params(2),
        name="nbr_attention",
    )(p, p, p, kv_ctx, kv_ctx, bias, p)


def _ctx_attn_kernel(q_ref, k_ref, v_ref, z_ref, o_ref, *, ctx_len):
    for q0 in range(0, ctx_len, GRID_W):
        for pk in range(N_PACKS):
            cols = slice(pk * PACK_W, (pk + 1) * PACK_W)
            qm = _stack_heads(q_ref[0, q0:q0 + GRID_W, cols])
            (p,), d = _softmax_weights([_nt_dot(qm, k_ref[0, :, cols])])
            o = jnp.dot(p, v_ref[0, :, cols], preferred_element_type=F32) / d
            o = _unstack_heads(o, GRID_W)
            z = z_ref[0, q0:q0 + GRID_W, cols].astype(F32)
            o_ref[0, q0:q0 + GRID_W, cols] = (o * z).astype(BF16)


def _ctx_attn_call(pc):
    nb, ctx_len, _ = pc.shape
    blk = lambda c: pl.BlockSpec((1, ctx_len, BRANCH_W), lambda b: (b, 0, c))
    return pl.pallas_call(
        functools.partial(_ctx_attn_kernel, ctx_len=ctx_len),
        grid=(nb,),
        in_specs=[blk(P_Q), blk(P_K), blk(P_V), blk(P_SCZ)],
        out_specs=blk(0),
        out_shape=jax.ShapeDtypeStruct((nb, ctx_len, BRANCH_W), BF16),
        compiler_params=_params(1),
        name="ctx_attention",
    )(pc, pc, pc, pc)


def _merge_kernel(x_ref, u_ref, up_ref, un_ref, abz_ref, yf_ref, yc_ref, ga_ref, gf_ref, gc_ref,
                  gt_ref, wcv_ref, bcv_ref, wa_ref, wf_ref, wc_ref, wo_ref, fg_ref, o_ref,
                  *, period, final):
    tm = u_ref.shape[1]
    u = u_ref[0].astype(F32)
    row = lax.broadcasted_iota(jnp.int32, (tm, 1), 0)
    pos = (pl.program_id(1) * tm + row) % period
    u_prev = pltpu.roll(u, 1, 0)
    u_prev = jnp.where(row == 0, up_ref[0, HALO_ROWS - 1:HALO_ROWS, :].astype(F32), u_prev)
    u_prev = jnp.where(pos == 0, 0.0, u_prev)
    u_next = pltpu.roll(u, tm - 1, 0)
    u_next = jnp.where(row == tm - 1, un_ref[0, 0:1, :].astype(F32), u_next)
    u_next = jnp.where(pos == period - 1, 0.0, u_next)
    conv = u_prev * wcv_ref[0:1, :] + u * wcv_ref[1:2, :] + u_next * wcv_ref[2:3, :] + bcv_ref[...]
    ya_in = (abz_ref[0].astype(F32) * conv).astype(BF16)

    y_a = jnp.dot(ya_in, wa_ref[...], preferred_element_type=F32)
    y_f = jnp.dot(yf_ref[0], wf_ref[...], preferred_element_type=F32)
    y_c = jnp.dot(yc_ref[0], wc_ref[...], preferred_element_type=F32)
    mix = (ga_ref[0].astype(F32) * y_a + gf_ref[0].astype(F32) * y_f
           + gc_ref[0].astype(F32) * y_c)
    z = jnp.dot(mix.astype(BF16), wo_ref[...], preferred_element_type=F32)
    xn = x_ref[0] + gt_ref[0] * z
    if final:
        xn = xn * lax.rsqrt(jnp.mean(xn * xn, axis=-1, keepdims=True) + EPS) * fg_ref[...]
    o_ref[0] = xn


def _merge_call(x, p, yf, yc, mods, mod_row0, w_conv, b_conv, w_a, w_f, w_c, w_o, final_g, layer,
                *, period, final):
    nb, s, _ = x.shape
    tm = TOKEN_TILE
    n_pairs = yf.shape[1]
    if yf.shape[3] == tm:
        def yf_index(b, i):
            top = i < n_pairs
            return (b, jnp.where(top, n_pairs - 1 - i, i - n_pairs), jnp.where(top, 0, 1), 0, 0)
    else:
        assert n_pairs == 1
        yf = yf.reshape(nb, 1, 1, s, BRANCH_W)
        yf_index = lambda b, i: (b, 0, 0, i, 0)
    halo_per_tile = tm // HALO_ROWS
    n_halo = s // HALO_ROWS
    tok = lambda width, c: pl.BlockSpec((1, tm, width), lambda b, i: (b, i, c))
    param = lambda a: pl.BlockSpec((None,) + a.shape[1:], lambda b, i: (layer, 0, 0))
    return pl.pallas_call(
        functools.partial(_merge_kernel, period=period, final=final),
        grid=(nb, s // tm),
        in_specs=[
            tok(D_MODEL, 0),
            tok(BRANCH_W, P_U),
            pl.BlockSpec((1, HALO_ROWS, BRANCH_W),
                         lambda b, i: (b, jnp.maximum(i * halo_per_tile - 1, 0), P_U)),
            pl.BlockSpec((1, HALO_ROWS, BRANCH_W),
                         lambda b, i: (b, jnp.minimum((i + 1) * halo_per_tile, n_halo - 1), P_U)),
            tok(BRANCH_W, P_ABZ),
            pl.BlockSpec((1, None, None, tm, BRANCH_W), yf_index),
            tok(BRANCH_W, 0),
            tok(D_MODEL, P_GATES // 2),
            tok(D_MODEL, P_GATES // 2 + 1),
            tok(D_MODEL, P_GATES // 2 + 2),
            pl.BlockSpec((None, 1, 1, D_MODEL), lambda b, i: (layer, b + mod_row0, 0, 2)),
            param(w_conv),
            param(b_conv),
            param(w_a),
            param(w_f),
            param(w_c),
            param(w_o),
            pl.BlockSpec((1, D_MODEL), lambda b, i: (0, 0)),
        ],
        out_specs=tok(D_MODEL, 0),
        out_shape=jax.ShapeDtypeStruct((nb, s, D_MODEL), F32),
        compiler_params=_params(2),
        name="merge_final" if final else "merge",
    )(x, p, p, p, p, yf, yc, p, p, p, mods, w_conv, b_conv, w_a, w_f, w_c, w_o, final_g)


def kernel(x, c, ctx, c_ctx, norm_g, w_ada, b_ada, w_in, b_in, w_conv, b_conv, rpb,
           w_br_a, w_br_f, w_br_c, w_out, final_g):
    nb, s, _ = x.shape
    ctx_len = ctx.shape[1]
    depth = w_in.shape[0]

    n_cond = -(-(nb + 1) // 8) * 8
    cc = jnp.zeros((n_cond, D_MODEL), F32).at[:nb].set(c).at[nb].set(c_ctx)
    mods = _mods_call(cc, w_ada, b_ada).reshape(depth, n_cond, 1, 3 * D_MODEL)

    q_cols = (jnp.arange(D_IN) >= Q_LO) & (jnp.arange(D_IN) < KV_LO)
    col_scale = jnp.where(q_cols, HEAD_DIM ** -0.5 * LOG2_E, 1.0).astype(F32)
    w_in_b = (w_in * col_scale).astype(BF16)
    b_in = (b_in * col_scale).reshape(depth, 1, D_IN)
    w_kv = w_in_b[depth - 1:, :, KV_LO:KV_HI]
    b_kv = b_in[depth - 1:, :, KV_LO:KV_HI]
    w_a, w_f, w_c, w_o = (w.astype(BF16) for w in (w_br_a, w_br_f, w_br_c, w_out))
    norm_g = norm_g.reshape(depth, 1, D_MODEL)
    b_conv = b_conv.reshape(depth, 1, BRANCH_W)
    final_g = final_g.reshape(1, D_MODEL)
    bias = _nbr_bias_table(rpb)
    tables_lat = _dft_tables(s)
    tables_ctx = _dft_tables(ctx_len)

    xc = ctx.reshape(1, nb * ctx_len, D_MODEL)
    for l in range(depth):
        last = l == depth - 1
        p_lat = _inproj_call(x, norm_g, mods, 0, w_in_b, b_in, l, l, full=True)
        if last:
            kv = _inproj_call(xc, norm_g, mods, nb, w_kv, b_kv, l, 0, full=False)
            kv_ctx, kc_blk, vc_blk = kv.reshape(nb, ctx_len, KV_HI - KV_LO), 0, 1
        else:
            p_ctx = _inproj_call(xc, norm_g, mods, nb, w_in_b, b_in, l, l, full=True)
            kv_ctx, kc_blk, vc_blk = p_ctx.reshape(nb, ctx_len, P_COLS), P_K, P_V

        yf = _fourier_call(p_lat, s, tables_lat)
        yc = _nbr_attn_call(p_lat, kv_ctx, kc_blk, vc_blk, bias, l)
        x = _merge_call(x, p_lat, yf, yc, mods, 0, w_conv, b_conv, w_a, w_f, w_c, w_o, final_g, l,
                        period=s, final=last)
        if not last:
            yf_c = _fourier_call(kv_ctx, ctx_len, tables_ctx)
            yc_c = _ctx_attn_call(kv_ctx).reshape(1, nb * ctx_len, BRANCH_W)
            xc = _merge_call(xc, p_ctx, yf_c, yc_c, mods, nb, w_conv, b_conv, w_a, w_f, w_c, w_o,
                             final_g, l, period=ctx_len, final=False)
    return x
```
